```python
import math
import jax, jax.numpy as jnp
from jax import lax
import numpy as np

D_MODEL = 2048
BATCH = 1
SEQ = 8192
DEPTH = 2
DEC_BATCH = 128
DEC_SEQ = 4
PAST_LEN = 16384
PAGE_SIZE = 128

N_EVEN = (DEPTH + 1) // 2
N_ODD = DEPTH // 2
D_FF = 4 * D_MODEL
EPS = 1e-6
Q_BLOCK = 128

MLA_HEADS = 8
MLA_NOPE = 128
MLA_ROPE = 64
MLA_V = 128
Q_LORA = 512
KV_LORA = 512
ROPE_THETA = 10000.0
MLA_SCALE = (MLA_NOPE + MLA_ROPE) ** -0.5

DIFF_HEADS = 8
DIFF_KV_HEADS = 2
DIFF_REP = DIFF_HEADS // DIFF_KV_HEADS
DIFF_HD = 64
DIFF_VD = 2 * DIFF_HD
DIFF_SCALE = DIFF_HD ** -0.5

IN0_SIZES = (Q_LORA, KV_LORA, MLA_ROPE, DIFF_HEADS * 2 * DIFF_HD,
             DIFF_KV_HEADS * 2 * DIFF_HD, DIFF_KV_HEADS * DIFF_VD)
IN0 = sum(IN0_SIZES)
MIX0 = MLA_HEADS * MLA_V + DIFF_HEADS * DIFF_VD

POOL_WINDOWS = (2, 4, 8, 16)
POOL_GROUP = 256
POOL_WIDTH = len(POOL_WINDOWS) * POOL_GROUP
POOL_BUF = max(POOL_WINDOWS) - 1

CHUNK = 128
GMLP_GROUPS = 4
GMLP_GROUP = 256
GMLP_WIDTH = GMLP_GROUPS * GMLP_GROUP

IN1 = POOL_WIDTH + 2 * GMLP_WIDTH
MIX1 = POOL_WIDTH + GMLP_WIDTH

kernel_name = 'hybrid_mla_diffattn_pool_gmlp_step'


def rms_norm(x, g):
    xf = x.astype(jnp.float32)
    y = xf * lax.rsqrt(jnp.mean(xf * xf, axis=-1, keepdims=True) + EPS)
    return (y * g.astype(jnp.float32)).astype(x.dtype)


def layer_norm(x, g):
    xf = x.astype(jnp.float32)
    mu = jnp.mean(xf, axis=-1, keepdims=True)
    xc = xf - mu
    y = xc * lax.rsqrt(jnp.mean(xc * xc, axis=-1, keepdims=True) + EPS)
    return (y * g.astype(jnp.float32)).astype(x.dtype)


def rope(x, pos):
    half = x.shape[-1] // 2
    inv = ROPE_THETA ** (-jnp.arange(half, dtype=jnp.float32) / half)
    ang = pos.astype(jnp.float32)[:, None] * inv[None, :]
    shape = (pos.shape[0],) + (1,) * (x.ndim - 3) + (half,)
    cos = jnp.cos(ang).reshape(shape)
    sin = jnp.sin(ang).reshape(shape)
    xf = x.astype(jnp.float32)
    x1, x2 = xf[..., :half], xf[..., half:]
    return jnp.concatenate([x1 * cos - x2 * sin, x1 * sin + x2 * cos], axis=-1).astype(x.dtype)


def sq_relu_mlp(h, w_up, w_down):
    a = jax.nn.relu(h @ w_up)
    return (a * a) @ w_down


def diff_lambda_value(lam_p, lam_init):
    lp = lam_p.astype(jnp.float32)
    return jnp.exp(jnp.sum(lp[0] * lp[1])) - jnp.exp(jnp.sum(lp[2] * lp[3])) + lam_init


def mla_diff_project(h, pos, w_in, q_norm, w_uq, kv_norm, w_uk):
    B, T, _ = h.shape
    z = h @ w_in
    c_q, c_kv, k_pe, dq, dk, dv = jnp.split(z, list(np.cumsum(IN0_SIZES)[:-1]), axis=-1)
    q = (rms_norm(c_q, q_norm) @ w_uq).reshape(B, T, MLA_HEADS, MLA_NOPE + MLA_ROPE)
    q_nope, q_pe = q[..., :MLA_NOPE], q[..., MLA_NOPE:]
    q_lat = jnp.einsum('bthn,rhn->bthr', q_nope, w_uk)
    q_pe = rope(q_pe, pos)
    ckv = rms_norm(c_kv, kv_norm)
    kpe = rope(k_pe, pos)
    dq = dq.reshape(B, T, DIFF_HEADS, 2 * DIFF_HD)
    dk = dk.reshape(B, T, DIFF_KV_HEADS, 2 * DIFF_HD)
    dv = dv.reshape(B, T, DIFF_KV_HEADS, DIFF_VD)
    return q_lat, q_pe, ckv, kpe, dq, dk, dv


def mla_attend(q_lat, q_pe, ckv, kpe, mask):
    s = (jnp.einsum('bthr,bsr->bhts', q_lat, ckv)
         + jnp.einsum('bthe,bse->bhts', q_pe, kpe)).astype(jnp.float32) * MLA_SCALE
    p = jax.nn.softmax(jnp.where(mask, s, -jnp.inf), axis=-1).astype(ckv.dtype)
    return jnp.einsum('bhts,bsr->bthr', p, ckv)


def diff_attend(dq, dk, dv, lam, mask):
    B, T = dq.shape[:2]
    q = dq.reshape(B, T, DIFF_KV_HEADS, DIFF_REP, 2, DIFF_HD)
    k = dk.reshape(B, dk.shape[1], DIFF_KV_HEADS, 2, DIFF_HD)
    s = jnp.einsum('btgrcd,bsgcd->cbgrts', q, k).astype(jnp.float32) * DIFF_SCALE
    p = jax.nn.softmax(jnp.where(mask, s, -jnp.inf), axis=-1)
    a = (p[0] - lam * p[1]).astype(dv.dtype)
    o = jnp.einsum('bgrts,bsgv->btgrv', a, dv)
    return o.reshape(B, T, DIFF_HEADS, DIFF_VD)


def prompt_attention(q_lat, q_pe, ckv, kpe, dq, dk, dv, lam):
    B, S = ckv.shape[:2]
    nblk = S // Q_BLOCK
    kpos = jnp.arange(S)

    def to_blocks(a):
        return jnp.moveaxis(a.reshape((B, nblk, Q_BLOCK) + a.shape[2:]), 1, 0)

    def from_blocks(a):
        return jnp.moveaxis(a, 0, 1).reshape((B, S) + a.shape[3:])

    def one_block(args):
        i, ql, qp, q = args
        qpos = i * Q_BLOCK + jnp.arange(Q_BLOCK)
        mask = kpos[None, :] <= qpos[:, None]
        return mla_attend(ql, qp, ckv, kpe, mask), diff_attend(q, dk, dv, lam, mask)

    o_lat, o_diff = lax.map(one_block, (jnp.arange(nblk), to_blocks(q_lat), to_blocks(q_pe), to_blocks(dq)))
    return from_blocks(o_lat), from_blocks(o_diff)


def sample_attention(q_lat, q_pe, ckv, kpe, dq, dk, dv, lam, layer,
                     cache_ckv, cache_kpe, cache_k, cache_v, page_table):
    T = q_lat.shape[1]
    past = page_table.shape[1] * cache_ckv.shape[2]
    kpos = jnp.arange(past + T)
    qpos = past + jnp.arange(T)
    mask = kpos[None, :] <= qpos[:, None]

    def one_seq(args):
        pt, ql, qp, c_new, kp_new, q, k_new, v_new = args

        def gather(pool, new):
            rows = pool[layer, pt]
            rows = rows.reshape((past,) + rows.shape[2:])
            return jnp.concatenate([rows.astype(new.dtype), new], axis=0)[None]

        o_lat = mla_attend(ql[None], qp[None], gather(cache_ckv, c_new), gather(cache_kpe, kp_new), mask)[0]
        o_diff = diff_attend(q[None], gather(cache_k, k_new), gather(cache_v, v_new), lam, mask)[0]
        return o_lat, o_diff

    return lax.map(one_seq, (page_table, q_lat, q_pe, ckv, kpe, dq, dk, dv))


def mla_diff_merge(o_lat, o_diff, w_uv, subln, lam_init, w_out):
    B, T = o_lat.shape[:2]
    o_mla = jnp.einsum('bthr,rhv->bthv', o_lat, w_uv).reshape(B, T, MLA_HEADS * MLA_V)
    o_d = (rms_norm(o_diff, subln) * (1.0 - lam_init)).reshape(B, T, DIFF_HEADS * DIFF_VD)
    return jnp.concatenate([o_mla, o_d], axis=-1) @ w_out


def pool_gmlp_project(h, w_in, gmlp_norm):
    z = h @ w_in
    hp = z[..., :POOL_WIDTH]
    uv = jax.nn.gelu(z[..., POOL_WIDTH:])
    u = uv[..., :GMLP_WIDTH]
    v = layer_norm(uv[..., GMLP_WIDTH:], gmlp_norm)
    return hp, u, v


def pool_mix(hp, buf, pos0, pool_w, pool_scale):
    B, T, P = hp.shape
    ext_raw = jnp.concatenate([buf.astype(hp.dtype), hp], axis=1)
    ext = ext_raw.astype(jnp.float32)
    cs = jnp.concatenate([jnp.zeros((B, 1, P), jnp.float32), jnp.cumsum(ext, axis=1)], axis=1)
    end = cs[:, POOL_BUF + 1:POOL_BUF + 1 + T]
    pos = pos0 + jnp.arange(T)
    outs = []
    for gi, w in enumerate(POOL_WINDOWS):
        sl = slice(gi * POOL_GROUP, (gi + 1) * POOL_GROUP)
        start = cs[:, POOL_BUF + 1 - w:POOL_BUF + 1 - w + T, sl]
        cnt = jnp.minimum(pos + 1, w).astype(jnp.float32)[None, :, None]
        d = ((end[..., sl] - start) / cnt - ext[:, POOL_BUF:, sl]).astype(hp.dtype)
        outs.append(d @ pool_w[gi])
    out = jnp.concatenate(outs, axis=-1) * pool_scale
    return out, ext_raw[:, -POOL_BUF:]


def gmlp_mix(u, v, ws, b):
    B, T, W = v.shape
    n = -(-T // CHUNK)
    pad = n * CHUNK - T
    vp = jnp.pad(v, ((0, 0), (0, pad), (0, 0))).reshape(B, n, CHUNK, GMLP_GROUPS, GMLP_GROUP)
    tril = jnp.tril(jnp.ones((CHUNK, CHUNK), dtype=bool))
    wm = jnp.where(tril[None], ws, jnp.zeros_like(ws))
    mixed = jnp.einsum('gij,bcjgd->bcigd', wm, vp) + b.T[None, None, :, :, None]
    mixed = mixed.reshape(B, n * CHUNK, W)[:, :T]
    return u * mixed


def setup_inputs(seed: int = 0) -> dict:
    key = jax.random.key(seed)
    ks = iter(jax.random.split(key, 40))
    f32 = jnp.float32

    def nrm(shape, scale=1.0):
        a = jax.random.normal(next(ks), shape, f32)
        return a if scale == 1.0 else a * scale

    def gain(shape):
        return 1.0 + 0.05 * nrm(shape)

    n_pages = PAST_LEN // PAGE_SIZE
    n_used = DEC_BATCH * n_pages
    n_pool = n_used + max(1, n_used // 4)
    page_table = jax.random.permutation(next(ks), n_pool)[:n_used].reshape(DEC_BATCH, n_pages).astype(jnp.int32)

    return {
        'x_prompt': nrm((BATCH, SEQ, D_MODEL)),
        'x_sample': nrm((DEC_BATCH, DEC_SEQ, D_MODEL)),
        'cache_mla_ckv': nrm((N_EVEN, n_pool, PAGE_SIZE, KV_LORA)),
        'cache_mla_kpe': nrm((N_EVEN, n_pool, PAGE_SIZE, MLA_ROPE)),
        'cache_diff_k': nrm((N_EVEN, n_pool, PAGE_SIZE, DIFF_KV_HEADS, 2 * DIFF_HD)),
        'cache_diff_v': nrm((N_EVEN, n_pool, PAGE_SIZE, DIFF_KV_HEADS, DIFF_VD)),
        'state_pool': nrm((N_ODD, DEC_BATCH, POOL_BUF, POOL_WIDTH)),
        'page_table': page_table,
        'norm_gains': gain((DEPTH, 4, D_MODEL)),
        'w_up': nrm((DEPTH, D_MODEL, D_FF), D_MODEL ** -0.5),
        'w_down': nrm((DEPTH, D_FF, D_MODEL), D_FF ** -0.5),
        'mla_diff_w_in': nrm((N_EVEN, D_MODEL, IN0), D_MODEL ** -0.5),
        'mla_q_norm': gain((N_EVEN, Q_LORA)),
        'mla_w_uq': nrm((N_EVEN, Q_LORA, MLA_HEADS * (MLA_NOPE + MLA_ROPE)), Q_LORA ** -0.5),
        'mla_kv_norm': gain((N_EVEN, KV_LORA)),
        'mla_w_uk': nrm((N_EVEN, KV_LORA, MLA_HEADS, MLA_NOPE), KV_LORA ** -0.5),
        'mla_w_uv': nrm((N_EVEN, KV_LORA, MLA_HEADS, MLA_V), KV_LORA ** -0.5),
        'diff_lambda': nrm((N_EVEN, 4, DIFF_HD), 0.1),
        'diff_subln': gain((N_EVEN, DIFF_VD)),
        'mla_diff_w_out': nrm((N_EVEN, MIX0, D_MODEL), MIX0 ** -0.5),
        'pool_gmlp_w_in': nrm((N_ODD, D_MODEL, IN1), D_MODEL ** -0.5),
        'pool_w': nrm((N_ODD, len(POOL_WINDOWS), POOL_GROUP, POOL_GROUP), POOL_GROUP ** -0.5),
        'pool_scale': gain((N_ODD, POOL_WIDTH)),
        'gmlp_norm': gain((N_ODD, GMLP_WIDTH)),
        'gmlp_ws': nrm((N_ODD, GMLP_GROUPS, CHUNK, CHUNK), CHUNK ** -0.5),
        'gmlp_b': 1.0 + 0.1 * nrm((N_ODD, GMLP_GROUPS, CHUNK)),
        'pool_gmlp_w_out': nrm((N_ODD, MIX1, D_MODEL), MIX1 ** -0.5),
    }


def reference(x_prompt, x_sample, cache_mla_ckv, cache_mla_kpe, cache_diff_k, cache_diff_v, state_pool,
              page_table, norm_gains, w_up, w_down, mla_diff_w_in, mla_q_norm, mla_w_uq, mla_kv_norm,
              mla_w_uk, mla_w_uv, diff_lambda, diff_subln, mla_diff_w_out, pool_gmlp_w_in, pool_w,
              pool_scale, gmlp_norm, gmlp_ws, gmlp_b, pool_gmlp_w_out):
    t_p = x_prompt.shape[1]
    t_s = x_sample.shape[1]
    past = page_table.shape[1] * cache_mla_ckv.shape[2]
    pos_p = jnp.arange(t_p, dtype=jnp.int32)
    pos_s = past + jnp.arange(t_s, dtype=jnp.int32)
    xp, xs = x_prompt, x_sample
    ckv_p, kpe_p, k_p, v_p, pool_p = [], [], [], [], []
    ckv_s, kpe_s, k_s, v_s, pool_s, gv_s = [], [], [], [], [], []
    for i in range(DEPTH):
        g = norm_gains[i]
        j = i // 2
        hp = rms_norm(xp, g[0])
        hs = rms_norm(xs, g[0])
        if i % 2 == 0:
            lam_init = 0.8 - 0.6 * math.exp(-0.3 * i)
            lam = diff_lambda_value(diff_lambda[j], lam_init)
            prj_p = mla_diff_project(hp, pos_p, mla_diff_w_in[j], mla_q_norm[j], mla_w_uq[j],
                                     mla_kv_norm[j], mla_w_uk[j])
            prj_s = mla_diff_project(hs, pos_s, mla_diff_w_in[j], mla_q_norm[j], mla_w_uq[j],
                                     mla_kv_norm[j], mla_w_uk[j])
            olat_p, od_p = prompt_attention(prj_p[0], prj_p[1], prj_p[2], prj_p[3], prj_p[4], prj_p[5],
                                            prj_p[6], lam)
            olat_s, od_s = sample_attention(prj_s[0], prj_s[1], prj_s[2], prj_s[3], prj_s[4], prj_s[5],
                                            prj_s[6], lam, j, cache_mla_ckv, cache_mla_kpe,
                                            cache_diff_k, cache_diff_v, page_table)
            mix_p = mla_diff_merge(olat_p, od_p, mla_w_uv[j], diff_subln[j], lam_init, mla_diff_w_out[j])
            mix_s = mla_diff_merge(olat_s, od_s, mla_w_uv[j], diff_subln[j], lam_init, mla_diff_w_out[j])
            ckv_p.append(prj_p[2]); kpe_p.append(prj_p[3]); k_p.append(prj_p[5]); v_p.append(prj_p[6])
            ckv_s.append(prj_s[2]); kpe_s.append(prj_s[3]); k_s.append(prj_s[5]); v_s.append(prj_s[6])
        else:
            hpp, up, vp = pool_gmlp_project(hp, pool_gmlp_w_in[j], gmlp_norm[j])
            hps, us, vs = pool_gmlp_project(hs, pool_gmlp_w_in[j], gmlp_norm[j])
            buf0 = jnp.zeros((hp.shape[0], POOL_BUF, POOL_WIDTH), hp.dtype)
            po_p, nb_p = pool_mix(hpp, buf0, 0, pool_w[j], pool_scale[j])
            po_s, nb_s = pool_mix(hps, state_pool[j], past, pool_w[j], pool_scale[j])
            go_p = gmlp_mix(up, vp, gmlp_ws[j], gmlp_b[j])
            go_s = gmlp_mix(us, vs, gmlp_ws[j], gmlp_b[j])
            mix_p = jnp.concatenate([po_p, go_p], axis=-1) @ pool_gmlp_w_out[j]
            mix_s = jnp.concatenate([po_s, go_s], axis=-1) @ pool_gmlp_w_out[j]
            pool_p.append(nb_p); pool_s.append(nb_s); gv_s.append(vs)
        xp = xp + rms_norm(mix_p, g[1])
        xs = xs + rms_norm(mix_s, g[1])
        xp = xp + rms_norm(sq_relu_mlp(rms_norm(xp, g[2]), w_up[i], w_down[i]), g[3])
        xs = xs + rms_norm(sq_relu_mlp(rms_norm(xs, g[2]), w_up[i], w_down[i]), g[3])
    return (xp, xs, jnp.stack(ckv_p), jnp.stack(kpe_p), jnp.stack(k_p), jnp.stack(v_p), jnp.stack(pool_p),
            jnp.stack(ckv_s), jnp.stack(kpe_s), jnp.stack(k_s), jnp.stack(v_s), jnp.stack(pool_s),
            jnp.stack(gv_s))
```

```python
import functools
import math

import jax
import jax.numpy as jnp
from jax import lax
from jax.experimental import pallas as pl
from jax.experimental.pallas import tpu as pltpu

F32 = jnp.float32
BF16 = jnp.bfloat16

EPS = 1e-6
ROPE_THETA = 10000.0
MLA_HEADS = 8
MLA_NOPE = 128
MLA_ROPE = 64
MLA_V = 128
Q_LORA = 512
KV_LORA = 512
MLA_SCALE = (MLA_NOPE + MLA_ROPE) ** -0.5
DIFF_HEADS = 8
DIFF_KV_HEADS = 2
DIFF_REP = DIFF_HEADS // DIFF_KV_HEADS
DIFF_HD = 64
DIFF_VD = 2 * DIFF_HD
DIFF_SCALE = DIFF_HD ** -0.5
POOL_WINDOWS = (2, 4, 8, 16)
POOL_GROUP = 256
POOL_WIDTH = len(POOL_WINDOWS) * POOL_GROUP
POOL_BUF = max(POOL_WINDOWS) - 1
CHUNK = 128
GMLP_GROUPS = 4
GMLP_GROUP = 256
GMLP_WIDTH = GMLP_GROUPS * GMLP_GROUP

LOG2E = 1.4426950408889634
NEG_BIG = -1e30
LANES = 128
V7X_VMEM_BUDGET = 56 * 1024 * 1024

_Z_CQ = 0
_Z_CKV = _Z_CQ + Q_LORA
_Z_DQ = _Z_CKV + KV_LORA
_Z_DK = _Z_DQ + DIFF_HEADS * 2 * DIFF_HD
_Z_DV = _Z_DK + DIFF_KV_HEADS * 2 * DIFF_HD
_Z_KPE = _Z_DV + DIFF_KV_HEADS * DIFF_VD
_Z_END = _Z_KPE + LANES


def _cparams(sem, vmem_bytes):
    return pltpu.CompilerParams(dimension_semantics=sem,
                                vmem_limit_bytes=int(min(vmem_bytes, V7X_VMEM_BUDGET)))


def _rms(x, g):
    return x * lax.rsqrt(jnp.mean(x * x, axis=-1, keepdims=True) + EPS) * g


def _dot(a, b):
    return jnp.dot(a, b, preferred_element_type=F32)


def _dot_nt(a, b):
    return lax.dot_general(a, b, (((1,), (1,)), ((), ())), preferred_element_type=F32)


def _rope128(x, cos, sin_signed, first_half):
    swapped = jnp.where(first_half, pltpu.roll(x, LANES - 32, axis=1), pltpu.roll(x, 32, axis=1))
    return x * cos + swapped * sin_signed


def _proj0_kernel(x_ref, g_ref, win_ref, qn_ref, wuq_ref, kvn_ref, wuk_ref, inv_ref,
                  qlat_ref, qpe_ref, ckv32_ref, ckv16_ref, kpe32_ref, kpe16_ref,
                  dq_ref, dk32_ref, dk16_ref, dv32_ref, dv16_ref, *, tm, seq, past, t_s):
    i = pl.program_id(0)
    h = _rms(x_ref[...], g_ref[...]).astype(BF16)
    z = _dot(h, win_ref[...])

    row = lax.broadcasted_iota(jnp.int32, (tm, LANES), 0) + i * tm
    pos = jnp.where(row < seq, row, past + lax.rem(row - seq, t_s)).astype(F32)
    ang = pos * inv_ref[...]
    cos = jnp.cos(ang)
    sin = jnp.sin(ang)
    lane = lax.broadcasted_iota(jnp.int32, (tm, LANES), 1)
    first_half = (lane & 32) == 0
    sin_signed = jnp.where(first_half, -sin, sin)

    ckv = _rms(z[:, _Z_CKV:_Z_CKV + KV_LORA], kvn_ref[...])
    ckv32_ref[...] = ckv
    ckv16_ref[...] = ckv.astype(BF16)
    kpe = _rope128(z[:, _Z_KPE:_Z_KPE + LANES], cos, sin_signed, first_half)[:, :MLA_ROPE]
    kpe32_ref[...] = kpe
    kpe16_ref[...] = kpe.astype(BF16)

    dq_ref[...] = (z[:, _Z_DQ:_Z_DK] * DIFF_SCALE).astype(BF16)
    dk = z[:, _Z_DK:_Z_DV]
    dk32_ref[...] = dk
    dk16_ref[...] = dk.astype(BF16)
    dv = z[:, _Z_DV:_Z_KPE]
    dv32_ref[...] = dv
    dv16_ref[...] = dv.astype(BF16)

    cq = _rms(z[:, _Z_CQ:_Z_CQ + Q_LORA], qn_ref[...]).astype(BF16)
    q = _dot(cq, wuq_ref[...])
    for hd in range(MLA_HEADS):
        qn = q[:, hd * MLA_NOPE:(hd + 1) * MLA_NOPE].astype(BF16)
        qlat_ref[hd] = _dot(qn, wuk_ref[hd]).astype(BF16)
    pe0 = MLA_HEADS * MLA_NOPE
    for c in range(MLA_HEADS * MLA_ROPE // LANES):
        r = _rope128(q[:, pe0 + c * LANES:pe0 + (c + 1) * LANES], cos, sin_signed, first_half).astype(BF16)
        qpe_ref[2 * c] = r[:, :MLA_ROPE]
        qpe_ref[2 * c + 1] = r[:, MLA_ROPE:]


def _proj0(x, g, win, qn, wuq, kvn, wuk_t, inv, *, seq, past, t_s, tm):
    n, d = x.shape
    grid = (n // tm,)
    const2 = lambda i: (0, 0)
    row2 = lambda i: (i, 0)
    out_shape = (
        jax.ShapeDtypeStruct((MLA_HEADS, n, KV_LORA), BF16),
        jax.ShapeDtypeStruct((MLA_HEADS, n, MLA_ROPE), BF16),
        jax.ShapeDtypeStruct((n, KV_LORA), F32),
        jax.ShapeDtypeStruct((n, KV_LORA), BF16),
        jax.ShapeDtypeStruct((n, MLA_ROPE), F32),
        jax.ShapeDtypeStruct((n, MLA_ROPE), BF16),
        jax.ShapeDtypeStruct((n, DIFF_HEADS * 2 * DIFF_HD), BF16),
        jax.ShapeDtypeStruct((n, DIFF_KV_HEADS * 2 * DIFF_HD), F32),
        jax.ShapeDtypeStruct((n, DIFF_KV_HEADS * 2 * DIFF_HD), BF16),
        jax.ShapeDtypeStruct((n, DIFF_KV_HEADS * DIFF_VD), F32),
        jax.ShapeDtypeStruct((n, DIFF_KV_HEADS * DIFF_VD), BF16),
    )
    out_specs = (
        pl.BlockSpec((MLA_HEADS, tm, KV_LORA), lambda i: (0, i, 0)),
        pl.BlockSpec((MLA_HEADS, tm, MLA_ROPE), lambda i: (0, i, 0)),
        pl.BlockSpec((tm, KV_LORA), row2),
        pl.BlockSpec((tm, KV_LORA), row2),
        pl.BlockSpec((tm, MLA_ROPE), row2),
        pl.BlockSpec((tm, MLA_ROPE), row2),
        pl.BlockSpec((tm, DIFF_HEADS * 2 * DIFF_HD), row2),
        pl.BlockSpec((tm, DIFF_KV_HEADS * 2 * DIFF_HD), row2),
        pl.BlockSpec((tm, DIFF_KV_HEADS * 2 * DIFF_HD), row2),
        pl.BlockSpec((tm, DIFF_KV_HEADS * DIFF_VD), row2),
        pl.BlockSpec((tm, DIFF_KV_HEADS * DIFF_VD), row2),
    )
    in_specs = [
        pl.BlockSpec((tm, d), row2),
        pl.BlockSpec((1, d), const2),
        pl.BlockSpec(win.shape, const2),
        pl.BlockSpec((1, Q_LORA), const2),
        pl.BlockSpec(wuq.shape, const2),
        pl.BlockSpec((1, KV_LORA), const2),
        pl.BlockSpec(wuk_t.shape, lambda i: (0, 0, 0)),
        pl.BlockSpec((1, LANES), const2),
    ]
    vmem = 2 * (win.size * 2 + wuq.size * 2 + wuk_t.size * 2) + 2 * tm * d * 4 + 12 * tm * _Z_END * 4
    return pl.pallas_call(
        functools.partial(_proj0_kernel, tm=tm, seq=seq, past=past, t_s=t_s),
        grid=grid, in_specs=in_specs, out_specs=out_specs, out_shape=out_shape,
        compiler_params=_cparams(("parallel",), vmem), name="proj0",
    )(x, g, win, qn, wuq, kvn, wuk_t, inv)


def _softmax_update(s2, v16, m_ref, l_ref, acc_ref):
    m_prev = m_ref[...]
    m_new = jnp.maximum(m_prev, jnp.max(s2, axis=-1, keepdims=True))
    p = jnp.exp2(s2 - m_new)
    alpha = jnp.exp2(m_prev - m_new)
    l_ref[...] = alpha * l_ref[...] + jnp.sum(p, axis=-1, keepdims=True)
    acc_ref[...] = alpha * acc_ref[...] + _dot(p.astype(BF16), v16)
    m_ref[...] = m_new


def _diff_lambda(lam_ref, lam_init):
    lp = lam_ref[...]
    a = jnp.sum(lp[0:1] * lp[1:2], axis=-1, keepdims=True)
    b = jnp.sum(lp[2:3] * lp[3:4], axis=-1, keepdims=True)
    return jnp.exp(a) - jnp.exp(b) + lam_init


def _mla_prompt_kernel(qlat_ref, qpe_ref, ckv_ref, kpe_ref, wuv_ref, o_ref,
                       m_ref, l_ref, acc_ref, *, tq, tk):
    i = pl.program_id(0)
    rows = MLA_HEADS * tq
    ql = qlat_ref[...].reshape(rows, KV_LORA)
    qp = qpe_ref[...].reshape(rows, MLA_ROPE)
    m_ref[...] = jnp.full((rows, 1), NEG_BIG, F32)
    l_ref[...] = jnp.zeros((rows, 1), F32)
    acc_ref[...] = jnp.zeros((rows, KV_LORA), F32)
    c = MLA_SCALE * LOG2E

    def scores(j):
        k0 = pl.multiple_of(j * tk, tk)
        k = ckv_ref[pl.ds(k0, tk), :]
        kp = kpe_ref[pl.ds(k0, tk), :]
        return (_dot_nt(ql, k) + _dot_nt(qp, kp)) * c, k

    def full_block(j, carry):
        s2, k = scores(j)
        _softmax_update(s2, k, m_ref, l_ref, acc_ref)
        return carry

    jd = (i * tq) // tk
    lax.fori_loop(0, jd, full_block, 0)
    s2, k = scores(jd)
    qpos = i * tq + (lax.broadcasted_iota(jnp.int32, (rows, tk), 0) & (tq - 1))
    kpos = jd * tk + lax.broadcasted_iota(jnp.int32, (rows, tk), 1)
    _softmax_update(jnp.where(kpos <= qpos, s2, NEG_BIG), k, m_ref, l_ref, acc_ref)

    o = (acc_ref[...] / l_ref[...]).astype(BF16)
    for hd in range(MLA_HEADS):
        o_ref[:, hd * MLA_V:(hd + 1) * MLA_V] = _dot(o[hd * tq:(hd + 1) * tq], wuv_ref[hd]).astype(BF16)


def _mla_prompt(qlat, qpe, ckv16, kpe16, wuv_t, *, seq, tq, tk):
    rows = MLA_HEADS * tq
    vmem = (2 * (seq * KV_LORA * 2 + seq * LANES * 2) + 4 * rows * KV_LORA * 2
            + rows * KV_LORA * 4 + 6 * rows * tk * 4 + 4 * wuv_t.size)
    return pl.pallas_call(
        functools.partial(_mla_prompt_kernel, tq=tq, tk=tk),
        grid=(seq // tq,),
        in_specs=[
            pl.BlockSpec((MLA_HEADS, tq, KV_LORA), lambda i: (0, i, 0)),
            pl.BlockSpec((MLA_HEADS, tq, MLA_ROPE), lambda i: (0, i, 0)),
            pl.BlockSpec((seq, KV_LORA), lambda i: (0, 0)),
            pl.BlockSpec((seq, MLA_ROPE), lambda i: (0, 0)),
            pl.BlockSpec(wuv_t.shape, lambda i: (0, 0, 0)),
        ],
        out_specs=pl.BlockSpec((tq, MLA_HEADS * MLA_V), lambda i: (i, 0)),
        out_shape=jax.ShapeDtypeStruct((seq, MLA_HEADS * MLA_V), BF16),
        scratch_shapes=[pltpu.VMEM((rows, 1), F32), pltpu.VMEM((rows, 1), F32),
                        pltpu.VMEM((rows, KV_LORA), F32)],
        compiler_params=_cparams(("parallel",), vmem), name="mla_prompt",
    )(qlat, qpe, ckv16, kpe16, wuv_t)


def _diff_finish(acc, l, lam, sub, scale_out):
    a = acc / l
    half = a.shape[0] // 2
    o = a[:half] - lam * a[half:]
    return _rms(o, sub) * scale_out


def _diff_prompt_kernel(dq_ref, dk_ref, dv_ref, lam_ref, sub_ref, o_ref,
                        q_scr, m_ref, l_ref, acc_ref, *, tq, tk, lam_init):
    i = pl.program_id(0)
    rows = 2 * DIFF_REP * tq
    lane = lax.broadcasted_iota(jnp.int32, (tq, 2 * DIFF_HD), 1)
    for g in range(DIFF_KV_HEADS):
        for r in range(DIFF_REP):
            hd = g * DIFF_REP + r
            qh = dq_ref[:, hd * 2 * DIFF_HD:(hd + 1) * 2 * DIFF_HD]
            zero = jnp.zeros_like(qh)
            q_scr[g, r * tq:(r + 1) * tq, :] = jnp.where(lane < DIFF_HD, qh, zero)
            q_scr[g, (DIFF_REP + r) * tq:(DIFF_REP + r + 1) * tq, :] = jnp.where(lane < DIFF_HD, zero, qh)
    m_ref[...] = jnp.full(m_ref.shape, NEG_BIG, F32)
    l_ref[...] = jnp.zeros(l_ref.shape, F32)
    acc_ref[...] = jnp.zeros(acc_ref.shape, F32)

    def block(j, masked):
        k0 = pl.multiple_of(j * tk, tk)
        for g in range(DIFF_KV_HEADS):
            k = dk_ref[pl.ds(k0, tk), g * 2 * DIFF_HD:(g + 1) * 2 * DIFF_HD]
            v = dv_ref[pl.ds(k0, tk), g * DIFF_VD:(g + 1) * DIFF_VD]
            s2 = _dot_nt(q_scr[g], k) * LOG2E
            if masked:
                qpos = i * tq + (lax.broadcasted_iota(jnp.int32, (rows, tk), 0) & (tq - 1))
                kpos = j * tk + lax.broadcasted_iota(jnp.int32, (rows, tk), 1)
                s2 = jnp.where(kpos <= qpos, s2, NEG_BIG)
            _softmax_update(s2, v, m_ref.at[g], l_ref.at[g], acc_ref.at[g])

    def full_block(j, carry):
        block(j, False)
        return carry

    jd = (i * tq) // tk
    lax.fori_loop(0, jd, full_block, 0)
    block(jd, True)

    lam = _diff_lambda(lam_ref, lam_init)
    for g in range(DIFF_KV_HEADS):
        o = _diff_finish(acc_ref[g], l_ref[g], lam, sub_ref[...], 1.0 - lam_init)
        for r in range(DIFF_REP):
            hd = g * DIFF_REP + r
            o_ref[:, hd * DIFF_VD:(hd + 1) * DIFF_VD] = o[r * tq:(r + 1) * tq].astype(BF16)


def _diff_prompt(dq16, dk16, dv16, lam_p, sub, *, seq, tq, tk, lam_init):
    rows = 2 * DIFF_REP * tq
    kw = DIFF_KV_HEADS * 2 * DIFF_HD
    vmem = (2 * 2 * seq * kw * 2 + 4 * tq * DIFF_HEADS * 2 * DIFF_HD * 2
            + DIFF_KV_HEADS * rows * (2 * DIFF_HD * 2 + DIFF_VD * 4 + 2 * LANES * 4) + 6 * rows * tk * 4)
    return pl.pallas_call(
        functools.partial(_diff_prompt_kernel, tq=tq, tk=tk, lam_init=lam_init),
        grid=(seq // tq,),
        in_specs=[
            pl.BlockSpec((tq, DIFF_HEADS * 2 * DIFF_HD), lambda i: (i, 0)),
            pl.BlockSpec((seq, kw), lambda i: (0, 0)),
            pl.BlockSpec((seq, DIFF_KV_HEADS * DIFF_VD), lambda i: (0, 0)),
            pl.BlockSpec(lam_p.shape, lambda i: (0, 0)),
            pl.BlockSpec(sub.shape, lambda i: (0, 0)),
        ],
        out_specs=pl.BlockSpec((tq, DIFF_HEADS * DIFF_VD), lambda i: (i, 0)),
        out_shape=jax.ShapeDtypeStruct((seq, DIFF_HEADS * DIFF_VD), BF16),
        scratch_shapes=[pltpu.VMEM((DIFF_KV_HEADS, rows, 2 * DIFF_HD), BF16),
                        pltpu.VMEM((DIFF_KV_HEADS, rows, 1), F32),
                        pltpu.VMEM((DIFF_KV_HEADS, rows, 1), F32),
                        pltpu.VMEM((DIFF_KV_HEADS, rows, DIFF_VD), F32)],
        compiler_params=_cparams(("parallel",), vmem), name="diff_prompt",
    )(dq16, dk16, dv16, lam_p, sub)


def _sample_attn_kernel(pt_ref, qlat_ref, qpe_ref, dq_ref, ckvn_ref, kpen_ref, dkn_ref, dvn_ref,
                        lam_ref, sub_ref, *rest, pages, t_s, lam_init):
    del pt_ref
    ckv_refs = rest[0:pages]
    kpe_refs = rest[pages:2 * pages]
    dk_refs = rest[2 * pages:3 * pages]
    dv_refs = rest[3 * pages:4 * pages]
    olat_ref, od_ref, m1_ref, l1_ref, acc1_ref, m2_ref, l2_ref, acc2_ref = rest[4 * pages:]
    s = pl.program_id(1)
    ql = qlat_ref[...]
    qp = qpe_ref[...]
    c = MLA_SCALE * LOG2E
    rows1 = ql.shape[0]
    rows2 = dq_ref.shape[1]

    @pl.when(s == 0)
    def _():
        m1_ref[...] = jnp.full(m1_ref.shape, NEG_BIG, F32)
        l1_ref[...] = jnp.zeros(l1_ref.shape, F32)
        acc1_ref[...] = jnp.zeros(acc1_ref.shape, F32)
        m2_ref[...] = jnp.full(m2_ref.shape, NEG_BIG, F32)
        l2_ref[...] = jnp.zeros(l2_ref.shape, F32)
        acc2_ref[...] = jnp.zeros(acc2_ref.shape, F32)
        kn = ckvn_ref[...]
        npad = kn.shape[0]
        s2 = (_dot_nt(ql, kn) + _dot_nt(qp, kpen_ref[...])) * c
        step1 = lax.rem(lax.broadcasted_iota(jnp.int32, (rows1, npad), 0), t_s)
        key1 = lax.broadcasted_iota(jnp.int32, (rows1, npad), 1)
        _softmax_update(jnp.where(key1 <= step1, s2, NEG_BIG), kn, m1_ref, l1_ref, acc1_ref)
        step2 = lax.rem(lax.broadcasted_iota(jnp.int32, (rows2, npad), 0), t_s)
        key2 = lax.broadcasted_iota(jnp.int32, (rows2, npad), 1)
        for g in range(DIFF_KV_HEADS):
            k = dkn_ref[:, g * 2 * DIFF_HD:(g + 1) * 2 * DIFF_HD]
            v = dvn_ref[:, g * DIFF_VD:(g + 1) * DIFF_VD]
            sd = _dot_nt(dq_ref[g], k) * LOG2E
            _softmax_update(jnp.where(key2 <= step2, sd, NEG_BIG), v, m2_ref.at[g], l2_ref.at[g], acc2_ref.at[g])

    k = jnp.concatenate([r[...].astype(BF16) for r in ckv_refs], axis=0)
    kp = jnp.concatenate([r[...].astype(BF16) for r in kpe_refs], axis=0)
    s2 = (_dot_nt(ql, k) + _dot_nt(qp, kp)) * c
    _softmax_update(s2, k, m1_ref, l1_ref, acc1_ref)
    dk = jnp.concatenate([r[...].astype(BF16) for r in dk_refs], axis=0)
    dv = jnp.concatenate([r[...].astype(BF16) for r in dv_refs], axis=0)
    for g in range(DIFF_KV_HEADS):
        sd = _dot_nt(dq_ref[g], dk[:, g * 2 * DIFF_HD:(g + 1) * 2 * DIFF_HD]) * LOG2E
        _softmax_update(sd, dv[:, g * DIFF_VD:(g + 1) * DIFF_VD], m2_ref.at[g], l2_ref.at[g], acc2_ref.at[g])

    @pl.when(s == pl.num_programs(1) - 1)
    def _():
        olat_ref[...] = (acc1_ref[...] / l1_ref[...]).astype(BF16)
        lam = _diff_lambda(lam_ref, lam_init)
        for g in range(DIFF_KV_HEADS):
            od_ref[g] = _diff_finish(acc2_ref[g], l2_ref[g], lam, sub_ref[...], 1.0 - lam_init).astype(BF16)


def _sample_attn(page_table, qlat_s, qpe_s, dq_s, ckvn, kpen, dkn, dvn, lam_p, sub,
                 cache_ckv, cache_kpe, cache_k, cache_v, *, layer, pages, t_s, lam_init):
    b, n_pages = page_table.shape
    page = cache_ckv.shape[2]
    steps = n_pages // pages
    rows1 = qlat_s.shape[1]
    rows2 = dq_s.shape[2]
    npad = ckvn.shape[1]
    kw = cache_k.shape[-1]
    vw = cache_v.shape[-1]

    def seq3(shape):
        return pl.BlockSpec((None,) + shape, lambda bi, si, pt: (bi, 0, 0))

    def page_spec(width, k):
        return pl.BlockSpec((None, None, page, width),
                            lambda bi, si, pt: (layer, pt[bi * n_pages + si * pages + k], 0, 0))

    in_specs = [
        seq3((rows1, KV_LORA)), seq3((rows1, MLA_ROPE)),
        pl.BlockSpec((None, DIFF_KV_HEADS, rows2, 2 * DIFF_HD), lambda bi, si, pt: (bi, 0, 0, 0)),
        seq3((npad, KV_LORA)), seq3((npad, MLA_ROPE)), seq3((npad, kw)), seq3((npad, vw)),
        pl.BlockSpec(lam_p.shape, lambda bi, si, pt: (0, 0)),
        pl.BlockSpec(sub.shape, lambda bi, si, pt: (0, 0)),
    ]
    in_specs += [page_spec(KV_LORA, k) for k in range(pages)]
    in_specs += [page_spec(MLA_ROPE, k) for k in range(pages)]
    in_specs += [page_spec(kw, k) for k in range(pages)]
    in_specs += [page_spec(vw, k) for k in range(pages)]
    out_specs = (
        seq3((rows1, KV_LORA)),
        pl.BlockSpec((None, DIFF_KV_HEADS, rows2 // 2, DIFF_VD), lambda bi, si, pt: (bi, 0, 0, 0)),
    )
    out_shape = (
        jax.ShapeDtypeStruct((b, rows1, KV_LORA), BF16),
        jax.ShapeDtypeStruct((b, DIFF_KV_HEADS, rows2 // 2, DIFF_VD), BF16),
    )
    scratch = [pltpu.VMEM((rows1, 1), F32), pltpu.VMEM((rows1, 1), F32), pltpu.VMEM((rows1, KV_LORA), F32),
               pltpu.VMEM((DIFF_KV_HEADS, rows2, 1), F32), pltpu.VMEM((DIFF_KV_HEADS, rows2, 1), F32),
               pltpu.VMEM((DIFF_KV_HEADS, rows2, DIFF_VD), F32)]
    page_bytes = page * (KV_LORA + LANES + kw + vw) * 4
    vmem = 2 * pages * page_bytes + 2 * pages * page_bytes + 4 * 1024 * 1024
    grid_spec = pltpu.PrefetchScalarGridSpec(
        num_scalar_prefetch=1, grid=(b, steps), in_specs=in_specs, out_specs=out_specs,
        scratch_shapes=scratch)
    return pl.pallas_call(
        functools.partial(_sample_attn_kernel, pages=pages, t_s=t_s, lam_init=lam_init),
        grid_spec=grid_spec, out_shape=out_shape,
        compiler_params=_cparams(("parallel", "arbitrary"), vmem), name="sample_attn",
    )(page_table.reshape(-1), qlat_s, qpe_s, dq_s, ckvn, kpen, dkn, dvn, lam_p, sub,
      *([cache_ckv] * pages), *([cache_kpe] * pages), *([cache_k] * pages), *([cache_v] * pages))


def _head_proj_kernel(o_ref, w_ref, out_ref):
    out_ref[...] = _dot(o_ref[...], w_ref[...]).astype(BF16)


def _head_proj(o_t, wuv_t):
    heads, rows, r = o_t.shape
    return pl.pallas_call(
        _head_proj_kernel, grid=(heads,),
        in_specs=[pl.BlockSpec((None, rows, r), lambda hd: (hd, 0, 0)),
                  pl.BlockSpec((None, r, MLA_V), lambda hd: (hd, 0, 0))],
        out_specs=pl.BlockSpec((rows, MLA_V), lambda hd: (0, hd)),
        out_shape=jax.ShapeDtypeStruct((rows, heads * MLA_V), BF16),
        compiler_params=_cparams(("parallel",), 16 * 1024 * 1024), name="head_proj",
    )(o_t, wuv_t)


def _out_proj_kernel(cat_ref, w_ref, g_ref, x_ref, o_ref):
    mix = _dot(cat_ref[...], w_ref[...])
    o_ref[...] = x_ref[...] + _rms(mix, g_ref[...])


def _out_proj(cat16, w16, g, x, *, tm):
    n, d = x.shape
    k = cat16.shape[1]
    vmem = 2 * k * d * 2 + 2 * tm * k * 2 + 6 * tm * d * 4
    return pl.pallas_call(
        _out_proj_kernel, grid=(n // tm,),
        in_specs=[pl.BlockSpec((tm, k), lambda i: (i, 0)),
                  pl.BlockSpec((k, d), lambda i: (0, 0)),
                  pl.BlockSpec((1, d), lambda i: (0, 0)),
                  pl.BlockSpec((tm, d), lambda i: (i, 0))],
        out_specs=pl.BlockSpec((tm, d), lambda i: (i, 0)),
        out_shape=jax.ShapeDtypeStruct((n, d), F32),
        compiler_params=_cparams(("parallel",), vmem), name="out_proj",
    )(cat16, w16, g, x)


def _mlp_kernel(x_ref, gpre_ref, gpost_ref, wup_ref, wdn_ref, o_ref, h_scr, acc_scr):
    k = pl.program_id(1)

    @pl.when(k == 0)
    def _():
        h_scr[...] = _rms(x_ref[...], gpre_ref[...]).astype(BF16)
        acc_scr[...] = jnp.zeros(acc_scr.shape, F32)

    a = jnp.maximum(_dot(h_scr[...], wup_ref[...]), 0.0)
    acc_scr[...] += _dot((a * a).astype(BF16), wdn_ref[...])

    @pl.when(k == pl.num_programs(1) - 1)
    def _():
        o_ref[...] = x_ref[...] + _rms(acc_scr[...], gpost_ref[...])


def _mlp(x, gpre, gpost, wup16, wdn16, *, tm, tf):
    n, d = x.shape
    f = wup16.shape[1]
    vmem = 2 * 2 * d * tf * 2 + 4 * tm * d * 4 + tm * d * 2 + tm * d * 4 + 3 * tm * tf * 4 + 2 * tm * d * 4
    return pl.pallas_call(
        _mlp_kernel, grid=(n // tm, f // tf),
        in_specs=[pl.BlockSpec((tm, d), lambda i, k: (i, 0)),
                  pl.BlockSpec((1, d), lambda i, k: (0, 0)),
                  pl.BlockSpec((1, d), lambda i, k: (0, 0)),
                  pl.BlockSpec((d, tf), lambda i, k: (0, k)),
                  pl.BlockSpec((tf, d), lambda i, k: (k, 0))],
        out_specs=pl.BlockSpec((tm, d), lambda i, k: (i, 0)),
        out_shape=jax.ShapeDtypeStruct((n, d), F32),
        scratch_shapes=[pltpu.VMEM((tm, d), BF16), pltpu.VMEM((tm, d), F32)],
        compiler_params=_cparams(("parallel", "arbitrary"), vmem), name="mlp",
    )(x, gpre, gpost, wup16, wdn16)


def _proj1_kernel(x_ref, g_ref, win_ref, gn_ref, hp_ref, u_ref, v32_ref, v16_ref):
    h = _rms(x_ref[...], g_ref[...]).astype(BF16)
    z = _dot(h, win_ref[...])
    hp_ref[...] = z[:, :POOL_WIDTH]
    uv = jax.nn.gelu(z[:, POOL_WIDTH:])
    u_ref[...] = uv[:, :GMLP_WIDTH]
    vv = uv[:, GMLP_WIDTH:]
    xc = vv - jnp.mean(vv, axis=-1, keepdims=True)
    v = xc * lax.rsqrt(jnp.mean(xc * xc, axis=-1, keepdims=True) + EPS) * gn_ref[...]
    v32_ref[...] = v
    v16_ref[...] = v.astype(BF16)


def _proj1(x, g, win16, gn, *, tm):
    n, d = x.shape
    w = win16.shape[1]
    row2 = lambda i: (i, 0)
    const2 = lambda i: (0, 0)
    vmem = 2 * win16.size * 2 + 2 * tm * d * 4 + 10 * tm * w * 4
    return pl.pallas_call(
        _proj1_kernel, grid=(n // tm,),
        in_specs=[pl.BlockSpec((tm, d), row2), pl.BlockSpec((1, d), const2),
                  pl.BlockSpec(win16.shape, const2), pl.BlockSpec((1, GMLP_WIDTH), const2)],
        out_specs=(pl.BlockSpec((tm, POOL_WIDTH), row2), pl.BlockSpec((tm, GMLP_WIDTH), row2),
                   pl.BlockSpec((tm, GMLP_WIDTH), row2), pl.BlockSpec((tm, GMLP_WIDTH), row2)),
        out_shape=(jax.ShapeDtypeStruct((n, POOL_WIDTH), F32), jax.ShapeDtypeStruct((n, GMLP_WIDTH), F32),
                   jax.ShapeDtypeStruct((n, GMLP_WIDTH), F32), jax.ShapeDtypeStruct((n, GMLP_WIDTH), BF16)),
        compiler_params=_cparams(("parallel",), vmem), name="proj1",
    )(x, g, win16, gn)


_HALO = 16


def _mix1_prompt_kernel(hp_ref, halo_ref, u_ref, v_ref, pw_ref, ps_ref, ws_ref, bt_ref, o_ref, *, tm):
    i = pl.program_id(0)
    halo = jnp.where(i == 0, 0.0, halo_ref[...])
    ext = jnp.concatenate([halo, hp_ref[...]], axis=0)
    pos = i * tm + lax.broadcasted_iota(jnp.int32, (tm, POOL_GROUP), 0)
    for gi, w in enumerate(POOL_WINDOWS):
        run = ext[:, gi * POOL_GROUP:(gi + 1) * POOL_GROUP]
        width = 1
        while width < w:
            run = run + pltpu.roll(run, width, axis=0)
            width *= 2
        wsum = run[_HALO:]
        cnt = jnp.minimum(pos + 1, w).astype(F32)
        d = (wsum / cnt - hp_ref[:, gi * POOL_GROUP:(gi + 1) * POOL_GROUP]).astype(BF16)
        po = _dot(d, pw_ref[gi]) * ps_ref[:, gi * POOL_GROUP:(gi + 1) * POOL_GROUP]
        o_ref[:, gi * POOL_GROUP:(gi + 1) * POOL_GROUP] = po.astype(BF16)

    tril = (lax.broadcasted_iota(jnp.int32, (CHUNK, CHUNK), 1)
            <= lax.broadcasted_iota(jnp.int32, (CHUNK, CHUNK), 0))
    for g in range(GMLP_GROUPS):
        wm = jnp.where(tril, ws_ref[g], 0.0).astype(BF16)
        bias = bt_ref[:, g:g + 1]
        for ch in range(tm // CHUNK):
            rows = slice(ch * CHUNK, (ch + 1) * CHUNK)
            cols = slice(g * GMLP_GROUP, (g + 1) * GMLP_GROUP)
            mixed = _dot(wm, v_ref[rows, cols]) + bias
            o_ref[rows, POOL_WIDTH + g * GMLP_GROUP:POOL_WIDTH + (g + 1) * GMLP_GROUP] = (
                u_ref[rows, cols] * mixed).astype(BF16)


def _mix1_prompt(hp, u, v16, pw16, ps, ws, b_t, *, seq, tm):
    halo_blocks = tm // _HALO
    row2 = lambda i: (i, 0)
    vmem = 24 * tm * POOL_WIDTH * 4 + 4 * 1024 * 1024
    return pl.pallas_call(
        functools.partial(_mix1_prompt_kernel, tm=tm), grid=(seq // tm,),
        in_specs=[pl.BlockSpec((tm, POOL_WIDTH), row2),
                  pl.BlockSpec((_HALO, POOL_WIDTH), lambda i: (jnp.maximum(i * halo_blocks - 1, 0), 0)),
                  pl.BlockSpec((tm, GMLP_WIDTH), row2),
                  pl.BlockSpec((tm, GMLP_WIDTH), row2),
                  pl.BlockSpec(pw16.shape, lambda i: (0, 0, 0)),
                  pl.BlockSpec(ps.shape, lambda i: (0, 0)),
                  pl.BlockSpec(ws.shape, lambda i: (0, 0, 0)),
                  pl.BlockSpec(b_t.shape, lambda i: (0, 0))],
        out_specs=pl.BlockSpec((tm, POOL_WIDTH + GMLP_WIDTH), row2),
        out_shape=jax.ShapeDtypeStruct((seq, POOL_WIDTH + GMLP_WIDTH), BF16),
        compiler_params=_cparams(("parallel",), vmem), name="mix1_prompt",
    )(hp, hp, u, v16, pw16, ps, ws, b_t)


def _mix1_sample_kernel(ws_ref, b_ref, st_ref, hp_ref, u_ref, v_ref, pw_ref, ps_ref, o_ref, *, t_s, past):
    for gi, w in enumerate(POOL_WINDOWS):
        cols = slice(gi * POOL_GROUP, (gi + 1) * POOL_GROUP)
        tails = [None] * w
        tail = None
        for k in range(1, w):
            row = st_ref[POOL_BUF - k, :, cols]
            tail = row if tail is None else tail + row
            tails[k] = tail
        for t in range(t_s):
            wsum = None
            for tp in range(max(0, t - w + 1), t + 1):
                r = hp_ref[tp, :, cols]
                wsum = r if wsum is None else wsum + r
            nbuf = max(w - 1 - t, 0)
            if nbuf > 0:
                wsum = wsum + tails[nbuf]
            cnt = float(min(past + t + 1, w))
            d = (wsum / cnt - hp_ref[t, :, cols]).astype(BF16)
            o_ref[t, :, cols] = (_dot(d, pw_ref[gi]) * ps_ref[:, cols]).astype(BF16)
    for g in range(GMLP_GROUPS):
        cols = slice(g * GMLP_GROUP, (g + 1) * GMLP_GROUP)
        for t in range(t_s):
            mixed = None
            for j in range(t + 1):
                term = ws_ref[g, t, j] * v_ref[j, :, cols].astype(F32)
                mixed = term if mixed is None else mixed + term
            mixed = mixed + b_ref[g, t]
            o_ref[t, :, POOL_WIDTH + g * GMLP_GROUP:POOL_WIDTH + (g + 1) * GMLP_GROUP] = (
                u_ref[t, :, cols] * mixed).astype(BF16)


def _mix1_sample(ws_small, b_small, st_t, hp_t, u_t, v16_t, pw16, ps, *, t_s, past):
    b = hp_t.shape[1]
    full = lambda a: pl.BlockSpec(a.shape, lambda i, n=a.ndim: (0,) * n)
    smem = lambda a: pl.BlockSpec(a.shape, lambda i, n=a.ndim: (0,) * n, memory_space=pltpu.SMEM)
    args = (ws_small, b_small, st_t, hp_t, u_t, v16_t, pw16, ps)
    in_specs = [smem(ws_small), smem(b_small)] + [full(a) for a in args[2:]]
    vmem = 4 * (st_t.size * 4 + 3 * hp_t.size * 4) + 8 * 1024 * 1024
    return pl.pallas_call(
        functools.partial(_mix1_sample_kernel, t_s=t_s, past=past), grid=(1,),
        in_specs=in_specs,
        out_specs=pl.BlockSpec((t_s, b, POOL_WIDTH + GMLP_WIDTH), lambda i: (0, 0, 0)),
        out_shape=jax.ShapeDtypeStruct((t_s, b, POOL_WIDTH + GMLP_WIDTH), BF16),
        compiler_params=_cparams(("arbitrary",), vmem), name="mix1_sample",
    )(*args)


def _pick_tile(n, prefs):
    for t in prefs:
        if n % t == 0:
            return t
    raise ValueError(f"no tile in {prefs} divides {n}")


def kernel(x_prompt, x_sample, cache_mla_ckv, cache_mla_kpe, cache_diff_k, cache_diff_v, state_pool, page_table, norm_gains, w_up, w_down, mla_diff_w_in, mla_q_norm, mla_w_uq, mla_kv_norm, mla_w_uk, mla_w_uv, diff_lambda, diff_subln, mla_diff_w_out, pool_gmlp_w_in, pool_w, pool_scale, gmlp_norm, gmlp_ws, gmlp_b, pool_gmlp_w_out):
    bp, seq, d = x_prompt.shape
    b, t_s, _ = x_sample.shape
    assert bp == 1, "one prompt sequence"
    n_pages = page_table.shape[1]
    page = cache_mla_ckv.shape[2]
    past = n_pages * page
    n = seq + b * t_s
    depth = norm_gains.shape[0]
    assert t_s <= 16 and (MLA_HEADS * t_s) % 16 == 0

    tm = _pick_tile(n, (256, 128))
    tm_mlp = _pick_tile(n, (512, 256, 128))
    tf = _pick_tile(w_up.shape[2], (1024, 512))
    tq = _pick_tile(seq, (128,))
    tk = _pick_tile(seq, (512, 256, 128))
    pages = _pick_tile(n_pages, (8, 4, 2, 1))

    x = jnp.concatenate([x_prompt.reshape(seq, d), x_sample.reshape(b * t_s, d)], axis=0)
    half = MLA_ROPE // 2
    inv = ROPE_THETA ** (-jnp.arange(half, dtype=F32) / half)
    inv_tab = jnp.tile(inv, LANES // half).reshape(1, LANES)

    outs = {}
    for i in range(depth):
        g = norm_gains[i]
        j = i // 2
        if i % 2 == 0:
            lam_init = 0.8 - 0.6 * math.exp(-0.3 * i)
            w_in = mla_diff_w_in[j]
            o_cq, o_ckv, o_kpe = 0, Q_LORA, Q_LORA + KV_LORA
            o_dq = o_kpe + MLA_ROPE
            o_dk = o_dq + DIFF_HEADS * 2 * DIFF_HD
            o_dv = o_dk + DIFF_KV_HEADS * 2 * DIFF_HD
            win16 = jnp.concatenate(
                [w_in[:, o_cq:o_ckv], w_in[:, o_ckv:o_kpe], w_in[:, o_dq:], w_in[:, o_kpe:o_dq],
                 jnp.zeros((d, LANES - MLA_ROPE), F32)], axis=1).astype(BF16)
            wuq = mla_w_uq[j].reshape(Q_LORA, MLA_HEADS, MLA_NOPE + MLA_ROPE)
            wuq16 = jnp.concatenate([wuq[:, :, :MLA_NOPE].reshape(Q_LORA, -1),
                                     wuq[:, :, MLA_NOPE:].reshape(Q_LORA, -1)], axis=1).astype(BF16)
            wuk_t = jnp.transpose(mla_w_uk[j], (1, 2, 0)).astype(BF16)
            wuv_t = jnp.transpose(mla_w_uv[j], (1, 0, 2)).astype(BF16)
            (qlat, qpe, ckv32, ckv16, kpe32, kpe16, dq16, dk32, dk16, dv32, dv16) = _proj0(
                x, g[0:1], win16, mla_q_norm[j].reshape(1, -1), wuq16, mla_kv_norm[j].reshape(1, -1),
                wuk_t, inv_tab, seq=seq, past=past, t_s=t_s, tm=tm)
            lam_p = diff_lambda[j]
            sub = diff_subln[j].reshape(1, -1)

            omla_p = _mla_prompt(qlat, qpe, ckv16, kpe16, wuv_t, seq=seq, tq=tq, tk=tk)
            od_p = _diff_prompt(dq16, dk16, dv16, lam_p, sub, seq=seq, tq=tq, tk=tk, lam_init=lam_init)

            qlat_s = jnp.transpose(qlat[:, seq:].reshape(MLA_HEADS, b, t_s, KV_LORA), (1, 0, 2, 3)).reshape(b, MLA_HEADS * t_s, KV_LORA)
            qpe_s = jnp.transpose(qpe[:, seq:].reshape(MLA_HEADS, b, t_s, MLA_ROPE), (1, 0, 2, 3)).reshape(b, MLA_HEADS * t_s, MLA_ROPE)
            dq_r = dq16[seq:].reshape(b, t_s, DIFF_KV_HEADS, DIFF_REP, 2, DIFF_HD)
            dq_r = jnp.transpose(dq_r, (0, 2, 4, 3, 1, 5))
            zq = jnp.zeros_like(dq_r[:, :, 0])
            dq_s = jnp.stack([jnp.concatenate([dq_r[:, :, 0], zq], axis=-1),
                              jnp.concatenate([zq, dq_r[:, :, 1]], axis=-1)], axis=2)
            dq_s = dq_s.reshape(b, DIFF_KV_HEADS, 2 * DIFF_REP * t_s, 2 * DIFF_HD)
            npad = 16

            def new_rows(a):
                a = a[seq:].reshape(b, t_s, a.shape[-1])
                return jnp.pad(a, ((0, 0), (0, npad - t_s), (0, 0)))

            n_pool = cache_diff_k.shape[1]
            olat_s, od_s = _sample_attn(
                page_table, qlat_s, qpe_s, dq_s, new_rows(ckv16), new_rows(kpe16), new_rows(dk16), new_rows(dv16),
                lam_p, sub, cache_mla_ckv, cache_mla_kpe,
                cache_diff_k.reshape(cache_diff_k.shape[0], n_pool, page, -1),
                cache_diff_v.reshape(cache_diff_v.shape[0], n_pool, page, -1),
                layer=j, pages=pages, t_s=t_s, lam_init=lam_init)
            olat_t = jnp.transpose(olat_s.reshape(b, MLA_HEADS, t_s, KV_LORA), (1, 0, 2, 3)).reshape(MLA_HEADS, b * t_s, KV_LORA)
            omla_s = _head_proj(olat_t, wuv_t)
            od_s = jnp.transpose(od_s.reshape(b, DIFF_KV_HEADS, DIFF_REP, t_s, DIFF_VD), (0, 3, 1, 2, 4)).reshape(b * t_s, DIFF_HEADS * DIFF_VD)
            cat = jnp.concatenate([jnp.concatenate([omla_p, od_p], axis=1),
                                   jnp.concatenate([omla_s, od_s], axis=1)], axis=0)
            w_out16 = mla_diff_w_out[j].astype(BF16)

            outs.setdefault("ckv_p", []).append(ckv32[:seq].reshape(bp, seq, KV_LORA))
            outs.setdefault("kpe_p", []).append(kpe32[:seq].reshape(bp, seq, MLA_ROPE))
            outs.setdefault("k_p", []).append(dk32[:seq].reshape(bp, seq, DIFF_KV_HEADS, 2 * DIFF_HD))
            outs.setdefault("v_p", []).append(dv32[:seq].reshape(bp, seq, DIFF_KV_HEADS, DIFF_VD))
            outs.setdefault("ckv_s", []).append(ckv32[seq:].reshape(b, t_s, KV_LORA))
            outs.setdefault("kpe_s", []).append(kpe32[seq:].reshape(b, t_s, MLA_ROPE))
            outs.setdefault("k_s", []).append(dk32[seq:].reshape(b, t_s, DIFF_KV_HEADS, 2 * DIFF_HD))
            outs.setdefault("v_s", []).append(dv32[seq:].reshape(b, t_s, DIFF_KV_HEADS, DIFF_VD))
        else:
            hp, u, v32, v16 = _proj1(x, g[0:1], pool_gmlp_w_in[j].astype(BF16), gmlp_norm[j].reshape(1, -1), tm=tm)
            pw16 = pool_w[j].astype(BF16)
            ps = pool_scale[j].reshape(1, -1)
            tm1 = _pick_tile(seq, (256, 128))
            cat_p = _mix1_prompt(hp, u, v16, pw16, ps, gmlp_ws[j], gmlp_b[j].T, seq=seq, tm=tm1)

            def steps_first(a):
                return jnp.transpose(a[seq:].reshape(b, t_s, a.shape[-1]), (1, 0, 2))

            ws_small = gmlp_ws[j][:, :t_s, :t_s].astype(BF16).astype(F32)
            cat_s = _mix1_sample(ws_small, gmlp_b[j][:, :t_s], jnp.transpose(state_pool[j], (1, 0, 2)),
                                 steps_first(hp), steps_first(u), steps_first(v16), pw16, ps, t_s=t_s, past=past)
            cat = jnp.concatenate([cat_p, jnp.transpose(cat_s, (1, 0, 2)).reshape(b * t_s, -1)], axis=0)
            w_out16 = pool_gmlp_w_out[j].astype(BF16)

            hp_s = hp[seq:].reshape(b, t_s, POOL_WIDTH)
            outs.setdefault("pool_p", []).append(hp[seq - POOL_BUF:seq].reshape(bp, POOL_BUF, POOL_WIDTH))
            outs.setdefault("pool_s", []).append(jnp.concatenate([state_pool[j], hp_s], axis=1)[:, -POOL_BUF:])
            outs.setdefault("gv_s", []).append(v32[seq:].reshape(b, t_s, GMLP_WIDTH))

        x = _out_proj(cat, w_out16, g[1:2], x, tm=tm)
        x = _mlp(x, g[2:3], g[3:4], w_up[i].astype(BF16), w_down[i].astype(BF16), tm=tm_mlp, tf=tf)

    st = lambda key: jnp.stack(outs[key])
    return (x[:seq].reshape(bp, seq, d), x[seq:].reshape(b, t_s, d),
            st("ckv_p"), st("kpe_p"), st("k_p"), st("v_p"), st("pool_p"),
            st("ckv_s"), st("kpe_s"), st("k_s"), st("v_s"), st("pool_s"), st("gv_s"))
```

```python
import functools
import math

import jax
import jax.numpy as jnp
from jax import lax
from jax.experimental import pallas as pl
from jax.experimental.pallas import tpu as pltpu

F32 = jnp.float32
BF16 = jnp.bfloat16

EPS = 1e-6
ROPE_THETA = 10000.0
MLA_HEADS = 8
MLA_NOPE = 128
MLA_ROPE = 64
MLA_V = 128
Q_LORA = 512
KV_LORA = 512
MLA_SCALE = (MLA_NOPE + MLA_ROPE) ** -0.5
DIFF_HEADS = 8
DIFF_KV_HEADS = 2
DIFF_REP = DIFF_HEADS // DIFF_KV_HEADS
DIFF_HD = 64
DIFF_VD = 2 * DIFF_HD
DIFF_SCALE = DIFF_HD ** -0.5
POOL_WINDOWS = (2, 4, 8, 16)
POOL_GROUP = 256
POOL_WIDTH = len(POOL_WINDOWS) * POOL_GROUP
POOL_BUF = max(POOL_WINDOWS) - 1
CHUNK = 128
GMLP_GROUPS = 4
GMLP_GROUP = 256
GMLP_WIDTH = GMLP_GROUPS * GMLP_GROUP

LOG2E = 1.4426950408889634
NEG_BIG = -1e30
LANES = 128
V7X_VMEM_BUDGET = 56 * 1024 * 1024

_Z_CQ = 0
_Z_CKV = _Z_CQ + Q_LORA
_Z_DQ = _Z_CKV + KV_LORA
_Z_DK = _Z_DQ + DIFF_HEADS * 2 * DIFF_HD
_Z_DV = _Z_DK + DIFF_KV_HEADS * 2 * DIFF_HD
_Z_KPE = _Z_DV + DIFF_KV_HEADS * DIFF_VD
_Z_END = _Z_KPE + LANES


def _cparams(sem, vmem_bytes):
    return pltpu.CompilerParams(dimension_semantics=sem,
                                vmem_limit_bytes=int(min(vmem_bytes, V7X_VMEM_BUDGET)))


def _rms(x, g):
    return x * lax.rsqrt(jnp.mean(x * x, axis=-1, keepdims=True) + EPS) * g


def _dot(a, b):
    return jnp.dot(a, b, preferred_element_type=F32)


def _dot_nt(a, b):
    return lax.dot_general(a, b, (((1,), (1,)), ((), ())), preferred_element_type=F32)


def _rope128(x, cos, sin_signed, first_half):
    swapped = jnp.where(first_half, pltpu.roll(x, LANES - 32, axis=1), pltpu.roll(x, 32, axis=1))
    return x * cos + swapped * sin_signed


def _proj0_kernel(x_ref, g_ref, win_ref, qn_ref, wuq_ref, kvn_ref, wuk_ref, inv_ref,
                  qlat_ref, qpe_ref, ckv32_ref, ckv16_ref, kpe32_ref, kpe16_ref,
                  dq_ref, dk32_ref, dk16_ref, dv32_ref, dv16_ref, *, tm, seq, past, t_s):
    i = pl.program_id(0)
    h = _rms(x_ref[...], g_ref[...]).astype(BF16)
    z = _dot(h, win_ref[...])

    row = lax.broadcasted_iota(jnp.int32, (tm, LANES), 0) + i * tm
    pos = jnp.where(row < seq, row, past + lax.rem(row - seq, t_s)).astype(F32)
    ang = pos * inv_ref[...]
    cos = jnp.cos(ang)
    sin = jnp.sin(ang)
    lane = lax.broadcasted_iota(jnp.int32, (tm, LANES), 1)
    first_half = (lane & 32) == 0
    sin_signed = jnp.where(first_half, -sin, sin)

    ckv = _rms(z[:, _Z_CKV:_Z_CKV + KV_LORA], kvn_ref[...])
    ckv32_ref[...] = ckv
    ckv16_ref[...] = ckv.astype(BF16)
    kpe = _rope128(z[:, _Z_KPE:_Z_KPE + LANES], cos, sin_signed, first_half)[:, :MLA_ROPE]
    kpe32_ref[...] = kpe
    kpe16_ref[...] = kpe.astype(BF16)

    dq_ref[...] = (z[:, _Z_DQ:_Z_DK] * DIFF_SCALE).astype(BF16)
    dk = z[:, _Z_DK:_Z_DV]
    dk32_ref[...] = dk
    dk16_ref[...] = dk.astype(BF16)
    dv = z[:, _Z_DV:_Z_KPE]
    dv32_ref[...] = dv
    dv16_ref[...] = dv.astype(BF16)

    cq = _rms(z[:, _Z_CQ:_Z_CQ + Q_LORA], qn_ref[...]).astype(BF16)
    q = _dot(cq, wuq_ref[...])
    for hd in range(MLA_HEADS):
        qn = q[:, hd * MLA_NOPE:(hd + 1) * MLA_NOPE].astype(BF16)
        qlat_ref[hd] = _dot(qn, wuk_ref[hd]).astype(BF16)
    pe0 = MLA_HEADS * MLA_NOPE
    for c in range(MLA_HEADS * MLA_ROPE // LANES):
        r = _rope128(q[:, pe0 + c * LANES:pe0 + (c + 1) * LANES], cos, sin_signed, first_half).astype(BF16)
        qpe_ref[2 * c] = r[:, :MLA_ROPE]
        qpe_ref[2 * c + 1] = r[:, MLA_ROPE:]


def _proj0(x, g, win, qn, wuq, kvn, wuk_t, inv, *, seq, past, t_s, tm):
    n, d = x.shape
    grid = (n // tm,)
    const2 = lambda i: (0, 0)
    row2 = lambda i: (i, 0)
    out_shape = (
        jax.ShapeDtypeStruct((MLA_HEADS, n, KV_LORA), BF16),
        jax.ShapeDtypeStruct((MLA_HEADS, n, MLA_ROPE), BF16),
        jax.ShapeDtypeStruct((n, KV_LORA), F32),
        jax.ShapeDtypeStruct((n, KV_LORA), BF16),
        jax.ShapeDtypeStruct((n, MLA_ROPE), F32),
        jax.ShapeDtypeStruct((n, MLA_ROPE), BF16),
        jax.ShapeDtypeStruct((n, DIFF_HEADS * 2 * DIFF_HD), BF16),
        jax.ShapeDtypeStruct((n, DIFF_KV_HEADS * 2 * DIFF_HD), F32),
        jax.ShapeDtypeStruct((n, DIFF_KV_HEADS * 2 * DIFF_HD), BF16),
        jax.ShapeDtypeStruct((n, DIFF_KV_HEADS * DIFF_VD), F32),
        jax.ShapeDtypeStruct((n, DIFF_KV_HEADS * DIFF_VD), BF16),
    )
    out_specs = (
        pl.BlockSpec((MLA_HEADS, tm, KV_LORA), lambda i: (0, i, 0)),
        pl.BlockSpec((MLA_HEADS, tm, MLA_ROPE), lambda i: (0, i, 0)),
        pl.BlockSpec((tm, KV_LORA), row2),
        pl.BlockSpec((tm, KV_LORA), row2),
        pl.BlockSpec((tm, MLA_ROPE), row2),
        pl.BlockSpec((tm, MLA_ROPE), row2),
        pl.BlockSpec((tm, DIFF_HEADS * 2 * DIFF_HD), row2),
        pl.BlockSpec((tm, DIFF_KV_HEADS * 2 * DIFF_HD), row2),
        pl.BlockSpec((tm, DIFF_KV_HEADS * 2 * DIFF_HD), row2),
        pl.BlockSpec((tm, DIFF_KV_HEADS * DIFF_VD), row2),
        pl.BlockSpec((tm, DIFF_KV_HEADS * DIFF_VD), row2),
    )
    in_specs = [
        pl.BlockSpec((tm, d), row2),
        pl.BlockSpec((1, d), const2),
        pl.BlockSpec(win.shape, const2),
        pl.BlockSpec((1, Q_LORA), const2),
        pl.BlockSpec(wuq.shape, const2),
        pl.BlockSpec((1, KV_LORA), const2),
        pl.BlockSpec(wuk_t.shape, lambda i: (0, 0, 0)),
        pl.BlockSpec((1, LANES), const2),
    ]
    vmem = 2 * (win.size * 2 + wuq.size * 2 + wuk_t.size * 2) + 2 * tm * d * 4 + 12 * tm * _Z_END * 4
    return pl.pallas_call(
        functools.partial(_proj0_kernel, tm=tm, seq=seq, past=past, t_s=t_s),
        grid=grid, in_specs=in_specs, out_specs=out_specs, out_shape=out_shape,
        compiler_params=_cparams(("parallel",), vmem), name="proj0",
    )(x, g, win, qn, wuq, kvn, wuk_t, inv)


def _lanes_to(x, width):
    if width <= LANES:
        return x[:, :width]
    return jnp.concatenate([x] * (width // LANES), axis=1)


def _softmax_update(s2, v16, m_ref, l_ref, acc_ref):
    m_prev = m_ref[...]
    m_new = jnp.maximum(m_prev, jnp.max(s2, axis=-1, keepdims=True))
    p = jnp.exp2(s2 - _lanes_to(m_new, s2.shape[1]))
    alpha = jnp.exp2(m_prev - m_new)
    l_ref[...] = alpha * l_ref[...] + jnp.sum(p, axis=-1, keepdims=True)
    acc_ref[...] = _lanes_to(alpha, acc_ref.shape[-1]) * acc_ref[...] + _dot(p.astype(BF16), v16)
    m_ref[...] = m_new


def _softmax_finish(acc_ref, l_ref):
    return acc_ref[...] / _lanes_to(l_ref[...], acc_ref.shape[-1])


def _diff_lambda(lam_ref, lam_init):
    lp = lam_ref[...]
    a = jnp.sum(lp[0:1] * lp[1:2], axis=-1, keepdims=True)
    b = jnp.sum(lp[2:3] * lp[3:4], axis=-1, keepdims=True)
    return jnp.exp(a) - jnp.exp(b) + lam_init


def _mla_prompt_kernel(qlat_ref, qpe_ref, ckv_ref, kpe_ref, wuv_ref, o_ref,
                       m_ref, l_ref, acc_ref, *, tq, tk):
    i = pl.program_id(0)
    rows = MLA_HEADS * tq
    ql = qlat_ref[...].reshape(rows, KV_LORA)
    qp = qpe_ref[...].reshape(rows, MLA_ROPE)
    m_ref[...] = jnp.full(m_ref.shape, NEG_BIG, F32)
    l_ref[...] = jnp.zeros(l_ref.shape, F32)
    acc_ref[...] = jnp.zeros(acc_ref.shape, F32)
    c = MLA_SCALE * LOG2E

    def scores(j):
        k0 = pl.multiple_of(j * tk, tk)
        k = ckv_ref[pl.ds(k0, tk), :]
        kp = kpe_ref[pl.ds(k0, tk), :]
        return (_dot_nt(ql, k) + _dot_nt(qp, kp)) * c, k

    def full_block(j, carry):
        s2, k = scores(j)
        _softmax_update(s2, k, m_ref, l_ref, acc_ref)
        return carry

    jd = (i * tq) // tk
    lax.fori_loop(0, jd, full_block, 0)
    s2, k = scores(jd)
    qpos = i * tq + (lax.broadcasted_iota(jnp.int32, (rows, tk), 0) & (tq - 1))
    kpos = jd * tk + lax.broadcasted_iota(jnp.int32, (rows, tk), 1)
    _softmax_update(jnp.where(kpos <= qpos, s2, NEG_BIG), k, m_ref, l_ref, acc_ref)

    o = _softmax_finish(acc_ref, l_ref).astype(BF16)
    for hd in range(MLA_HEADS):
        o_ref[:, hd * MLA_V:(hd + 1) * MLA_V] = _dot(o[hd * tq:(hd + 1) * tq], wuv_ref[hd]).astype(BF16)


def _mla_prompt(qlat, qpe, ckv16, kpe16, wuv_t, *, seq, tq, tk):
    rows = MLA_HEADS * tq
    vmem = (2 * (seq * KV_LORA * 2 + seq * LANES * 2) + 4 * rows * KV_LORA * 2
            + rows * KV_LORA * 4 + 6 * rows * tk * 4 + 4 * wuv_t.size)
    return pl.pallas_call(
        functools.partial(_mla_prompt_kernel, tq=tq, tk=tk),
        grid=(seq // tq,),
        in_specs=[
            pl.BlockSpec((MLA_HEADS, tq, KV_LORA), lambda i: (0, i, 0)),
            pl.BlockSpec((MLA_HEADS, tq, MLA_ROPE), lambda i: (0, i, 0)),
            pl.BlockSpec((seq, KV_LORA), lambda i: (0, 0)),
            pl.BlockSpec((seq, MLA_ROPE), lambda i: (0, 0)),
            pl.BlockSpec(wuv_t.shape, lambda i: (0, 0, 0)),
        ],
        out_specs=pl.BlockSpec((tq, MLA_HEADS * MLA_V), lambda i: (i, 0)),
        out_shape=jax.ShapeDtypeStruct((seq, MLA_HEADS * MLA_V), BF16),
        scratch_shapes=[pltpu.VMEM((rows, LANES), F32), pltpu.VMEM((rows, LANES), F32),
                        pltpu.VMEM((rows, KV_LORA), F32)],
        compiler_params=_cparams(("parallel",), vmem), name="mla_prompt",
    )(qlat, qpe, ckv16, kpe16, wuv_t)


def _diff_finish(a, lam, sub, scale_out):
    half = a.shape[0] // 2
    o = a[:half] - lam * a[half:]
    return _rms(o, sub) * scale_out


def _diff_prompt_kernel(dq_ref, dk_ref, dv_ref, lam_ref, sub_ref, o_ref,
                        q_scr, m_ref, l_ref, acc_ref, *, tq, tk, lam_init):
    i = pl.program_id(0)
    rows = 2 * DIFF_REP * tq
    lane = lax.broadcasted_iota(jnp.int32, (tq, 2 * DIFF_HD), 1)
    for g in range(DIFF_KV_HEADS):
        for r in range(DIFF_REP):
            hd = g * DIFF_REP + r
            qh = dq_ref[:, hd * 2 * DIFF_HD:(hd + 1) * 2 * DIFF_HD]
            zero = jnp.zeros_like(qh)
            q_scr[g, r * tq:(r + 1) * tq, :] = jnp.where(lane < DIFF_HD, qh, zero)
            q_scr[g, (DIFF_REP + r) * tq:(DIFF_REP + r + 1) * tq, :] = jnp.where(lane < DIFF_HD, zero, qh)
    m_ref[...] = jnp.full(m_ref.shape, NEG_BIG, F32)
    l_ref[...] = jnp.zeros(l_ref.shape, F32)
    acc_ref[...] = jnp.zeros(acc_ref.shape, F32)

    def block(j, masked):
        k0 = pl.multiple_of(j * tk, tk)
        for g in range(DIFF_KV_HEADS):
            k = dk_ref[pl.ds(k0, tk), g * 2 * DIFF_HD:(g + 1) * 2 * DIFF_HD]
            v = dv_ref[pl.ds(k0, tk), g * DIFF_VD:(g + 1) * DIFF_VD]
            s2 = _dot_nt(q_scr[g], k) * LOG2E
            if masked:
                qpos = i * tq + (lax.broadcasted_iota(jnp.int32, (rows, tk), 0) & (tq - 1))
                kpos = j * tk + lax.broadcasted_iota(jnp.int32, (rows, tk), 1)
                s2 = jnp.where(kpos <= qpos, s2, NEG_BIG)
            _softmax_update(s2, v, m_ref.at[g], l_ref.at[g], acc_ref.at[g])

    def full_block(j, carry):
        block(j, False)
        return carry

    jd = (i * tq) // tk
    lax.fori_loop(0, jd, full_block, 0)
    block(jd, True)

    lam = _diff_lambda(lam_ref, lam_init)
    for g in range(DIFF_KV_HEADS):
        o = _diff_finish(_softmax_finish(acc_ref.at[g], l_ref.at[g]), lam, sub_ref[...], 1.0 - lam_init)
        for r in range(DIFF_REP):
            hd = g * DIFF_REP + r
            o_ref[:, hd * DIFF_VD:(hd + 1) * DIFF_VD] = o[r * tq:(r + 1) * tq].astype(BF16)


def _diff_prompt(dq16, dk16, dv16, lam_p, sub, *, seq, tq, tk, lam_init):
    rows = 2 * DIFF_REP * tq
    kw = DIFF_KV_HEADS * 2 * DIFF_HD
    vmem = (2 * 2 * seq * kw * 2 + 4 * tq * DIFF_HEADS * 2 * DIFF_HD * 2
            + DIFF_KV_HEADS * rows * (2 * DIFF_HD * 2 + DIFF_VD * 4 + 2 * LANES * 4) + 6 * rows * tk * 4)
    return pl.pallas_call(
        functools.partial(_diff_prompt_kernel, tq=tq, tk=tk, lam_init=lam_init),
        grid=(seq // tq,),
        in_specs=[
            pl.BlockSpec((tq, DIFF_HEADS * 2 * DIFF_HD), lambda i: (i, 0)),
            pl.BlockSpec((seq, kw), lambda i: (0, 0)),
            pl.BlockSpec((seq, DIFF_KV_HEADS * DIFF_VD), lambda i: (0, 0)),
            pl.BlockSpec(lam_p.shape, lambda i: (0, 0)),
            pl.BlockSpec(sub.shape, lambda i: (0, 0)),
        ],
        out_specs=pl.BlockSpec((tq, DIFF_HEADS * DIFF_VD), lambda i: (i, 0)),
        out_shape=jax.ShapeDtypeStruct((seq, DIFF_HEADS * DIFF_VD), BF16),
        scratch_shapes=[pltpu.VMEM((DIFF_KV_HEADS, rows, 2 * DIFF_HD), BF16),
                        pltpu.VMEM((DIFF_KV_HEADS, rows, LANES), F32),
                        pltpu.VMEM((DIFF_KV_HEADS, rows, LANES), F32),
                        pltpu.VMEM((DIFF_KV_HEADS, rows, DIFF_VD), F32)],
        compiler_params=_cparams(("parallel",), vmem), name="diff_prompt",
    )(dq16, dk16, dv16, lam_p, sub)


_N_CACHES = 4


def _sample_attn_kernel(pt_ref, qlat_ref, qpe_ref, dq_ref, ckvn_ref, kpen_ref, dkn_ref, dvn_ref,
                        lam_ref, sub_ref, ckv_hbm, kpe_hbm, dk_hbm, dv_hbm, olat_ref, od_ref,
                        ckv_buf, kpe_buf, dk_buf, dv_buf, sems,
                        m1_ref, l1_ref, acc1_ref, m2_ref, l2_ref, acc2_ref, *, pages, layer, t_s, lam_init):
    steps = pl.num_programs(1)
    s = pl.program_id(1)
    n = pl.program_id(0) * steps + s
    total = pl.num_programs(0) * steps
    slot = lax.rem(n, 2)

    def page_copies(step, slot_):
        cps = []
        for kk in range(pages):
            pid = pt_ref[step * pages + kk]
            for ci, (hbm, buf) in enumerate(((ckv_hbm, ckv_buf), (kpe_hbm, kpe_buf),
                                             (dk_hbm, dk_buf), (dv_hbm, dv_buf))):
                cps.append(pltpu.make_async_copy(hbm.at[layer, pid], buf.at[slot_, kk], sems.at[slot_, ci]))
        return cps

    @pl.when(n == 0)
    def _():
        for cp in page_copies(0, 0):
            cp.start()

    @pl.when(n + 1 < total)
    def _():
        for cp in page_copies(n + 1, 1 - slot):
            cp.start()

    ql = qlat_ref[...]
    qp = qpe_ref[...]
    dq = dq_ref[...]
    c = MLA_SCALE * LOG2E
    rows1 = ql.shape[0]
    rows2 = dq.shape[0]
    grows = rows2 // DIFF_KV_HEADS

    def group_mask(keys):
        grp = lax.broadcasted_iota(jnp.int32, (rows2, keys), 0) // grows
        return (lax.broadcasted_iota(jnp.int32, (rows2, keys), 1) & (DIFF_KV_HEADS - 1)) == grp

    @pl.when(s == 0)
    def _():
        m1_ref[...] = jnp.full(m1_ref.shape, NEG_BIG, F32)
        l1_ref[...] = jnp.zeros(l1_ref.shape, F32)
        acc1_ref[...] = jnp.zeros(acc1_ref.shape, F32)
        m2_ref[...] = jnp.full(m2_ref.shape, NEG_BIG, F32)
        l2_ref[...] = jnp.zeros(l2_ref.shape, F32)
        acc2_ref[...] = jnp.zeros(acc2_ref.shape, F32)
        kn = ckvn_ref[...]
        npad = kn.shape[0]
        s2 = (_dot_nt(ql, kn) + _dot_nt(qp, kpen_ref[...])) * c
        step1 = lax.rem(lax.broadcasted_iota(jnp.int32, (rows1, npad), 0), t_s)
        key1 = lax.broadcasted_iota(jnp.int32, (rows1, npad), 1)
        _softmax_update(jnp.where(key1 <= step1, s2, NEG_BIG), kn, m1_ref, l1_ref, acc1_ref)
        dkn = dkn_ref[...]
        nk = dkn.shape[0]
        sd = _dot_nt(dq, dkn) * LOG2E
        step2 = lax.rem(lax.broadcasted_iota(jnp.int32, (rows2, nk), 0), t_s)
        key2 = lax.broadcasted_iota(jnp.int32, (rows2, nk), 1) // DIFF_KV_HEADS
        ok = jnp.logical_and(group_mask(nk), key2 <= step2)
        _softmax_update(jnp.where(ok, sd, NEG_BIG), dvn_ref[...], m2_ref, l2_ref, acc2_ref)

    for cp in page_copies(n, slot):
        cp.wait()

    k = jnp.concatenate([ckv_buf[slot, kk].astype(BF16) for kk in range(pages)], axis=0)
    kp_t = jnp.concatenate([kpe_buf[slot, kk].astype(BF16) for kk in range(pages)], axis=1)
    s2 = (_dot_nt(ql, k) + _dot(qp, kp_t)) * c
    _softmax_update(s2, k, m1_ref, l1_ref, acc1_ref)
    dk = jnp.concatenate([dk_buf[slot, kk].astype(BF16) for kk in range(pages)], axis=0)
    dv = jnp.concatenate([dv_buf[slot, kk].astype(BF16) for kk in range(pages)], axis=0)
    sd = _dot_nt(dq, dk) * LOG2E
    _softmax_update(jnp.where(group_mask(dk.shape[0]), sd, NEG_BIG), dv, m2_ref, l2_ref, acc2_ref)

    @pl.when(s == steps - 1)
    def _():
        olat_ref[...] = _softmax_finish(acc1_ref, l1_ref).astype(BF16)
        lam = _diff_lambda(lam_ref, lam_init)
        a = _softmax_finish(acc2_ref, l2_ref)
        for g in range(DIFF_KV_HEADS):
            od_ref[g] = _diff_finish(a[g * grows:(g + 1) * grows], lam, sub_ref[...], 1.0 - lam_init).astype(BF16)


def _sample_attn(page_table, qlat_s, qpe_s, dq_s, ckvn, kpen, dkn, dvn, lam_p, sub,
                 cache_ckv, cache_kpe, cache_k, cache_v, *, layer, pages, t_s, lam_init):
    b, n_pages = page_table.shape
    assert n_pages % pages == 0
    steps = n_pages // pages
    rows1 = qlat_s.shape[1]
    rows2 = dq_s.shape[1]
    grows = rows2 // DIFF_KV_HEADS

    def seq3(a):
        return pl.BlockSpec((None,) + a.shape[1:], lambda bi, si, pt: (bi, 0, 0))

    hbm = pl.BlockSpec(memory_space=pl.ANY)
    in_specs = [
        seq3(qlat_s), seq3(qpe_s), seq3(dq_s), seq3(ckvn), seq3(kpen), seq3(dkn), seq3(dvn),
        pl.BlockSpec(lam_p.shape, lambda bi, si, pt: (0, 0)),
        pl.BlockSpec(sub.shape, lambda bi, si, pt: (0, 0)),
        hbm, hbm, hbm, hbm,
    ]
    out_specs = (
        pl.BlockSpec((None, rows1, KV_LORA), lambda bi, si, pt: (bi, 0, 0)),
        pl.BlockSpec((None, DIFF_KV_HEADS, grows // 2, DIFF_VD), lambda bi, si, pt: (bi, 0, 0, 0)),
    )
    out_shape = (
        jax.ShapeDtypeStruct((b, rows1, KV_LORA), BF16),
        jax.ShapeDtypeStruct((b, DIFF_KV_HEADS, grows // 2, DIFF_VD), BF16),
    )
    bufs = [pltpu.VMEM((2, pages) + c.shape[2:], c.dtype) for c in (cache_ckv, cache_kpe, cache_k, cache_v)]
    scratch = bufs + [
        pltpu.SemaphoreType.DMA((2, _N_CACHES)),
        pltpu.VMEM((rows1, LANES), F32), pltpu.VMEM((rows1, LANES), F32), pltpu.VMEM((rows1, KV_LORA), F32),
        pltpu.VMEM((rows2, LANES), F32), pltpu.VMEM((rows2, LANES), F32), pltpu.VMEM((rows2, DIFF_VD), F32)]
    step_bytes = pages * sum(math.prod(c.shape[2:]) for c in (cache_ckv, cache_kpe, cache_k, cache_v)) * 4
    vmem = 2 * step_bytes + 2 * step_bytes + 6 * 1024 * 1024
    grid_spec = pltpu.PrefetchScalarGridSpec(
        num_scalar_prefetch=1, grid=(b, steps), in_specs=in_specs, out_specs=out_specs,
        scratch_shapes=scratch)
    return pl.pallas_call(
        functools.partial(_sample_attn_kernel, pages=pages, layer=layer, t_s=t_s, lam_init=lam_init),
        grid_spec=grid_spec, out_shape=out_shape,
        compiler_params=_cparams(("arbitrary", "arbitrary"), vmem), name="sample_attn",
    )(page_table.reshape(-1), qlat_s, qpe_s, dq_s, ckvn, kpen, dkn, dvn, lam_p, sub,
      cache_ckv, cache_kpe, cache_k, cache_v)


def _head_proj_kernel(o_ref, w_ref, out_ref):
    out_ref[...] = _dot(o_ref[...], w_ref[...]).astype(BF16)


def _head_proj(o_t, wuv_t):
    heads, rows, r = o_t.shape
    return pl.pallas_call(
        _head_proj_kernel, grid=(heads,),
        in_specs=[pl.BlockSpec((None, rows, r), lambda hd: (hd, 0, 0)),
                  pl.BlockSpec((None, r, MLA_V), lambda hd: (hd, 0, 0))],
        out_specs=pl.BlockSpec((rows, MLA_V), lambda hd: (0, hd)),
        out_shape=jax.ShapeDtypeStruct((rows, heads * MLA_V), BF16),
        compiler_params=_cparams(("parallel",), 16 * 1024 * 1024), name="head_proj",
    )(o_t, wuv_t)


def _out_proj_kernel(cat_ref, w_ref, g_ref, x_ref, o_ref):
    mix = _dot(cat_ref[...], w_ref[...])
    o_ref[...] = x_ref[...] + _rms(mix, g_ref[...])


def _out_proj(cat16, w16, g, x, *, tm):
    n, d = x.shape
    k = cat16.shape[1]
    vmem = 2 * k * d * 2 + 2 * tm * k * 2 + 6 * tm * d * 4
    return pl.pallas_call(
        _out_proj_kernel, grid=(n // tm,),
        in_specs=[pl.BlockSpec((tm, k), lambda i: (i, 0)),
                  pl.BlockSpec((k, d), lambda i: (0, 0)),
                  pl.BlockSpec((1, d), lambda i: (0, 0)),
                  pl.BlockSpec((tm, d), lambda i: (i, 0))],
        out_specs=pl.BlockSpec((tm, d), lambda i: (i, 0)),
        out_shape=jax.ShapeDtypeStruct((n, d), F32),
        compiler_params=_cparams(("parallel",), vmem), name="out_proj",
    )(cat16, w16, g, x)


def _mlp_kernel(x_ref, gpre_ref, gpost_ref, wup_ref, wdn_ref, o_ref, h_scr, acc_scr):
    k = pl.program_id(1)

    @pl.when(k == 0)
    def _():
        h_scr[...] = _rms(x_ref[...], gpre_ref[...]).astype(BF16)
        acc_scr[...] = jnp.zeros(acc_scr.shape, F32)

    a = jnp.maximum(_dot(h_scr[...], wup_ref[...]), 0.0)
    acc_scr[...] += _dot((a * a).astype(BF16), wdn_ref[...])

    @pl.when(k == pl.num_programs(1) - 1)
    def _():
        o_ref[...] = x_ref[...] + _rms(acc_scr[...], gpost_ref[...])


def _mlp(x, gpre, gpost, wup16, wdn16, *, tm, tf):
    n, d = x.shape
    f = wup16.shape[1]
    vmem = 2 * 2 * d * tf * 2 + 4 * tm * d * 4 + tm * d * 2 + tm * d * 4 + 3 * tm * tf * 4 + 2 * tm * d * 4
    return pl.pallas_call(
        _mlp_kernel, grid=(n // tm, f // tf),
        in_specs=[pl.BlockSpec((tm, d), lambda i, k: (i, 0)),
                  pl.BlockSpec((1, d), lambda i, k: (0, 0)),
                  pl.BlockSpec((1, d), lambda i, k: (0, 0)),
                  pl.BlockSpec((d, tf), lambda i, k: (0, k)),
                  pl.BlockSpec((tf, d), lambda i, k: (k, 0))],
        out_specs=pl.BlockSpec((tm, d), lambda i, k: (i, 0)),
        out_shape=jax.ShapeDtypeStruct((n, d), F32),
        scratch_shapes=[pltpu.VMEM((tm, d), BF16), pltpu.VMEM((tm, d), F32)],
        compiler_params=_cparams(("parallel", "arbitrary"), vmem), name="mlp",
    )(x, gpre, gpost, wup16, wdn16)


def _proj1_kernel(x_ref, g_ref, win_ref, gn_ref, hp_ref, u_ref, v32_ref, v16_ref):
    h = _rms(x_ref[...], g_ref[...]).astype(BF16)
    z = _dot(h, win_ref[...])
    hp_ref[...] = z[:, :POOL_WIDTH]
    uv = jax.nn.gelu(z[:, POOL_WIDTH:])
    u_ref[...] = uv[:, :GMLP_WIDTH]
    vv = uv[:, GMLP_WIDTH:]
    xc = vv - jnp.mean(vv, axis=-1, keepdims=True)
    v = xc * lax.rsqrt(jnp.mean(xc * xc, axis=-1, keepdims=True) + EPS) * gn_ref[...]
    v32_ref[...] = v
    v16_ref[...] = v.astype(BF16)


def _proj1(x, g, win16, gn, *, tm):
    n, d = x.shape
    w = win16.shape[1]
    row2 = lambda i: (i, 0)
    const2 = lambda i: (0, 0)
    vmem = 2 * win16.size * 2 + 2 * tm * d * 4 + 10 * tm * w * 4
    return pl.pallas_call(
        _proj1_kernel, grid=(n // tm,),
        in_specs=[pl.BlockSpec((tm, d), row2), pl.BlockSpec((1, d), const2),
                  pl.BlockSpec(win16.shape, const2), pl.BlockSpec((1, GMLP_WIDTH), const2)],
        out_specs=(pl.BlockSpec((tm, POOL_WIDTH), row2), pl.BlockSpec((tm, GMLP_WIDTH), row2),
                   pl.BlockSpec((tm, GMLP_WIDTH), row2), pl.BlockSpec((tm, GMLP_WIDTH), row2)),
        out_shape=(jax.ShapeDtypeStruct((n, POOL_WIDTH), F32), jax.ShapeDtypeStruct((n, GMLP_WIDTH), F32),
                   jax.ShapeDtypeStruct((n, GMLP_WIDTH), F32), jax.ShapeDtypeStruct((n, GMLP_WIDTH), BF16)),
        compiler_params=_cparams(("parallel",), vmem), name="proj1",
    )(x, g, win16, gn)


_HALO = 16


def _mix1_prompt_kernel(hp_ref, halo_ref, u_ref, v_ref, pw_ref, ps_ref, ws_ref, bt_ref, o_ref, *, tm):
    i = pl.program_id(0)
    halo = jnp.where(i == 0, 0.0, halo_ref[...])
    ext = jnp.concatenate([halo, hp_ref[...]], axis=0)
    pos = i * tm + lax.broadcasted_iota(jnp.int32, (tm, POOL_GROUP), 0)
    for gi, w in enumerate(POOL_WINDOWS):
        run = ext[:, gi * POOL_GROUP:(gi + 1) * POOL_GROUP]
        width = 1
        while width < w:
            run = run + pltpu.roll(run, width, axis=0)
            width *= 2
        wsum = run[_HALO:]
        cnt = jnp.minimum(pos + 1, w).astype(F32)
        d = (wsum / cnt - hp_ref[:, gi * POOL_GROUP:(gi + 1) * POOL_GROUP]).astype(BF16)
        po = _dot(d, pw_ref[gi]) * ps_ref[:, gi * POOL_GROUP:(gi + 1) * POOL_GROUP]
        o_ref[:, gi * POOL_GROUP:(gi + 1) * POOL_GROUP] = po.astype(BF16)

    tril = (lax.broadcasted_iota(jnp.int32, (CHUNK, CHUNK), 1)
            <= lax.broadcasted_iota(jnp.int32, (CHUNK, CHUNK), 0))
    for g in range(GMLP_GROUPS):
        wm = jnp.where(tril, ws_ref[g], 0.0).astype(BF16)
        bias = bt_ref[:, g:g + 1]
        for ch in range(tm // CHUNK):
            rows = slice(ch * CHUNK, (ch + 1) * CHUNK)
            cols = slice(g * GMLP_GROUP, (g + 1) * GMLP_GROUP)
            mixed = _dot(wm, v_ref[rows, cols]) + bias
            o_ref[rows, POOL_WIDTH + g * GMLP_GROUP:POOL_WIDTH + (g + 1) * GMLP_GROUP] = (
                u_ref[rows, cols] * mixed).astype(BF16)


def _mix1_prompt(hp, u, v16, pw16, ps, ws, b_t, *, seq, tm):
    halo_blocks = tm // _HALO
    row2 = lambda i: (i, 0)
    vmem = 24 * tm * POOL_WIDTH * 4 + 4 * 1024 * 1024
    return pl.pallas_call(
        functools.partial(_mix1_prompt_kernel, tm=tm), grid=(seq // tm,),
        in_specs=[pl.BlockSpec((tm, POOL_WIDTH), row2),
                  pl.BlockSpec((_HALO, POOL_WIDTH), lambda i: (jnp.maximum(i * halo_blocks - 1, 0), 0)),
                  pl.BlockSpec((tm, GMLP_WIDTH), row2),
                  pl.BlockSpec((tm, GMLP_WIDTH), row2),
                  pl.BlockSpec(pw16.shape, lambda i: (0, 0, 0)),
                  pl.BlockSpec(ps.shape, lambda i: (0, 0)),
                  pl.BlockSpec(ws.shape, lambda i: (0, 0, 0)),
                  pl.BlockSpec(b_t.shape, lambda i: (0, 0))],
        out_specs=pl.BlockSpec((tm, POOL_WIDTH + GMLP_WIDTH), row2),
        out_shape=jax.ShapeDtypeStruct((seq, POOL_WIDTH + GMLP_WIDTH), BF16),
        compiler_params=_cparams(("parallel",), vmem), name="mix1_prompt",
    )(hp, hp, u, v16, pw16, ps, ws, b_t)


def _mix1_sample_kernel(ws_ref, b_ref, st_ref, hp_ref, u_ref, v_ref, pw_ref, ps_ref, o_ref, *, t_s, past):
    for gi, w in enumerate(POOL_WINDOWS):
        cols = slice(gi * POOL_GROUP, (gi + 1) * POOL_GROUP)
        tails = [None] * w
        tail = None
        for k in range(1, w):
            row = st_ref[POOL_BUF - k, :, cols]
            tail = row if tail is None else tail + row
            tails[k] = tail
        for t in range(t_s):
            wsum = None
            for tp in range(max(0, t - w + 1), t + 1):
                r = hp_ref[tp, :, cols]
                wsum = r if wsum is None else wsum + r
            nbuf = max(w - 1 - t, 0)
            if nbuf > 0:
                wsum = wsum + tails[nbuf]
            cnt = float(min(past + t + 1, w))
            d = (wsum / cnt - hp_ref[t, :, cols]).astype(BF16)
            o_ref[t, :, cols] = (_dot(d, pw_ref[gi]) * ps_ref[:, cols]).astype(BF16)
    for g in range(GMLP_GROUPS):
        cols = slice(g * GMLP_GROUP, (g + 1) * GMLP_GROUP)
        for t in range(t_s):
            mixed = None
            for j in range(t + 1):
                term = ws_ref[g, t, j] * v_ref[j, :, cols].astype(F32)
                mixed = term if mixed is None else mixed + term
            mixed = mixed + b_ref[g, t]
            o_ref[t, :, POOL_WIDTH + g * GMLP_GROUP:POOL_WIDTH + (g + 1) * GMLP_GROUP] = (
                u_ref[t, :, cols] * mixed).astype(BF16)


def _mix1_sample(ws_small, b_small, st_t, hp_t, u_t, v16_t, pw16, ps, *, t_s, past):
    b = hp_t.shape[1]
    full = lambda a: pl.BlockSpec(a.shape, lambda i, n=a.ndim: (0,) * n)
    smem = lambda a: pl.BlockSpec(a.shape, lambda i, n=a.ndim: (0,) * n, memory_space=pltpu.SMEM)
    args = (ws_small, b_small, st_t, hp_t, u_t, v16_t, pw16, ps)
    in_specs = [smem(ws_small), smem(b_small)] + [full(a) for a in args[2:]]
    vmem = 4 * (st_t.size * 4 + 3 * hp_t.size * 4) + 8 * 1024 * 1024
    return pl.pallas_call(
        functools.partial(_mix1_sample_kernel, t_s=t_s, past=past), grid=(1,),
        in_specs=in_specs,
        out_specs=pl.BlockSpec((t_s, b, POOL_WIDTH + GMLP_WIDTH), lambda i: (0, 0, 0)),
        out_shape=jax.ShapeDtypeStruct((t_s, b, POOL_WIDTH + GMLP_WIDTH), BF16),
        compiler_params=_cparams(("arbitrary",), vmem), name="mix1_sample",
    )(*args)


def _pick_tile(n, prefs):
    for t in prefs:
        if n % t == 0:
            return t
    raise ValueError(f"no tile in {prefs} divides {n}")


def kernel(x_prompt, x_sample, cache_mla_ckv, cache_mla_kpe, cache_diff_k, cache_diff_v, state_pool, page_table, norm_gains, w_up, w_down, mla_diff_w_in, mla_q_norm, mla_w_uq, mla_kv_norm, mla_w_uk, mla_w_uv, diff_lambda, diff_subln, mla_diff_w_out, pool_gmlp_w_in, pool_w, pool_scale, gmlp_norm, gmlp_ws, gmlp_b, pool_gmlp_w_out):
    bp, seq, d = x_prompt.shape
    b, t_s, _ = x_sample.shape
    assert bp == 1, "one prompt sequence"
    n_pages = page_table.shape[1]
    page = cache_mla_ckv.shape[2]
    past = n_pages * page
    n = seq + b * t_s
    depth = norm_gains.shape[0]
    assert t_s <= 16 and (MLA_HEADS * t_s) % 16 == 0

    tm = _pick_tile(n, (256, 128))
    tm_mlp = _pick_tile(n, (512, 256, 128))
    tf = _pick_tile(w_up.shape[2], (1024, 512))
    tq = _pick_tile(seq, (128,))
    tk = _pick_tile(seq, (512, 256, 128))
    pages = _pick_tile(n_pages, (16, 8, 4, 2, 1))

    x = jnp.concatenate([x_prompt.reshape(seq, d), x_sample.reshape(b * t_s, d)], axis=0)
    half = MLA_ROPE // 2
    inv = ROPE_THETA ** (-jnp.arange(half, dtype=F32) / half)
    inv_tab = jnp.tile(inv, LANES // half).reshape(1, LANES)

    outs = {}
    for i in range(depth):
        g = norm_gains[i]
        j = i // 2
        if i % 2 == 0:
            lam_init = 0.8 - 0.6 * math.exp(-0.3 * i)
            w_in = mla_diff_w_in[j]
            o_cq, o_ckv, o_kpe = 0, Q_LORA, Q_LORA + KV_LORA
            o_dq = o_kpe + MLA_ROPE
            o_dk = o_dq + DIFF_HEADS * 2 * DIFF_HD
            o_dv = o_dk + DIFF_KV_HEADS * 2 * DIFF_HD
            win16 = jnp.concatenate(
                [w_in[:, o_cq:o_ckv], w_in[:, o_ckv:o_kpe], w_in[:, o_dq:], w_in[:, o_kpe:o_dq],
                 jnp.zeros((d, LANES - MLA_ROPE), F32)], axis=1).astype(BF16)
            wuq = mla_w_uq[j].reshape(Q_LORA, MLA_HEADS, MLA_NOPE + MLA_ROPE)
            wuq16 = jnp.concatenate([wuq[:, :, :MLA_NOPE].reshape(Q_LORA, -1),
                                     wuq[:, :, MLA_NOPE:].reshape(Q_LORA, -1)], axis=1).astype(BF16)
            wuk_t = jnp.transpose(mla_w_uk[j], (1, 2, 0)).astype(BF16)
            wuv_t = jnp.transpose(mla_w_uv[j], (1, 0, 2)).astype(BF16)
            (qlat, qpe, ckv32, ckv16, kpe32, kpe16, dq16, dk32, dk16, dv32, dv16) = _proj0(
                x, g[0:1], win16, mla_q_norm[j].reshape(1, -1), wuq16, mla_kv_norm[j].reshape(1, -1),
                wuk_t, inv_tab, seq=seq, past=past, t_s=t_s, tm=tm)
            lam_p = diff_lambda[j]
            sub = diff_subln[j].reshape(1, -1)

            omla_p = _mla_prompt(qlat, qpe, ckv16, kpe16, wuv_t, seq=seq, tq=tq, tk=tk)
            od_p = _diff_prompt(dq16, dk16, dv16, lam_p, sub, seq=seq, tq=tq, tk=tk, lam_init=lam_init)

            qlat_s = jnp.transpose(qlat[:, seq:].reshape(MLA_HEADS, b, t_s, KV_LORA), (1, 0, 2, 3)).reshape(b, MLA_HEADS * t_s, KV_LORA)
            qpe_s = jnp.transpose(qpe[:, seq:].reshape(MLA_HEADS, b, t_s, MLA_ROPE), (1, 0, 2, 3)).reshape(b, MLA_HEADS * t_s, MLA_ROPE)
            dq_r = dq16[seq:].reshape(b, t_s, DIFF_KV_HEADS, DIFF_REP, 2, DIFF_HD)
            dq_r = jnp.transpose(dq_r, (0, 2, 4, 3, 1, 5))
            zq = jnp.zeros_like(dq_r[:, :, 0])
            dq_s = jnp.stack([jnp.concatenate([dq_r[:, :, 0], zq], axis=-1),
                              jnp.concatenate([zq, dq_r[:, :, 1]], axis=-1)], axis=2)
            dq_s = dq_s.reshape(b, DIFF_KV_HEADS * 2 * DIFF_REP * t_s, 2 * DIFF_HD)
            npad = 16

            def new_rows(a):
                a = a[seq:].reshape(b, t_s, a.shape[-1])
                return jnp.pad(a, ((0, 0), (0, npad - t_s), (0, 0)))

            def interleaved(a):
                return a.reshape(a.shape[:-2] + (a.shape[-2] * DIFF_KV_HEADS, a.shape[-1] // DIFF_KV_HEADS))

            n_pool = cache_diff_k.shape[1]
            olat_s, od_s = _sample_attn(
                page_table, qlat_s, qpe_s, dq_s, new_rows(ckv16), new_rows(kpe16),
                interleaved(new_rows(dk16)), interleaved(new_rows(dv16)),
                lam_p, sub, cache_mla_ckv, jnp.swapaxes(cache_mla_kpe, 2, 3),
                cache_diff_k.reshape(cache_diff_k.shape[0], n_pool, page * DIFF_KV_HEADS, 2 * DIFF_HD),
                cache_diff_v.reshape(cache_diff_v.shape[0], n_pool, page * DIFF_KV_HEADS, DIFF_VD),
                layer=j, pages=pages, t_s=t_s, lam_init=lam_init)
            olat_t = jnp.transpose(olat_s.reshape(b, MLA_HEADS, t_s, KV_LORA), (1, 0, 2, 3)).reshape(MLA_HEADS, b * t_s, KV_LORA)
            omla_s = _head_proj(olat_t, wuv_t)
            od_s = jnp.transpose(od_s.reshape(b, DIFF_KV_HEADS, DIFF_REP, t_s, DIFF_VD), (0, 3, 1, 2, 4)).reshape(b * t_s, DIFF_HEADS * DIFF_VD)
            cat = jnp.concatenate([jnp.concatenate([omla_p, od_p], axis=1),
                                   jnp.concatenate([omla_s, od_s], axis=1)], axis=0)
            w_out16 = mla_diff_w_out[j].astype(BF16)

            outs.setdefault("ckv_p", []).append(ckv32[:seq].reshape(bp, seq, KV_LORA))
            outs.setdefault("kpe_p", []).append(kpe32[:seq].reshape(bp, seq, MLA_ROPE))
            outs.setdefault("k_p", []).append(dk32[:seq].reshape(bp, seq, DIFF_KV_HEADS, 2 * DIFF_HD))
            outs.setdefault("v_p", []).append(dv32[:seq].reshape(bp, seq, DIFF_KV_HEADS, DIFF_VD))
            outs.setdefault("ckv_s", []).append(ckv32[seq:].reshape(b, t_s, KV_LORA))
            outs.setdefault("kpe_s", []).append(kpe32[seq:].reshape(b, t_s, MLA_ROPE))
            outs.setdefault("k_s", []).append(dk32[seq:].reshape(b, t_s, DIFF_KV_HEADS, 2 * DIFF_HD))
            outs.setdefault("v_s", []).append(dv32[seq:].reshape(b, t_s, DIFF_KV_HEADS, DIFF_VD))
        else:
            hp, u, v32, v16 = _proj1(x, g[0:1], pool_gmlp_w_in[j].astype(BF16), gmlp_norm[j].reshape(1, -1), tm=tm)
            pw16 = pool_w[j].astype(BF16)
            ps = pool_scale[j].reshape(1, -1)
            tm1 = _pick_tile(seq, (256, 128))
            cat_p = _mix1_prompt(hp, u, v16, pw16, ps, gmlp_ws[j], gmlp_b[j].T, seq=seq, tm=tm1)

            def steps_first(a):
                return jnp.transpose(a[seq:].reshape(b, t_s, a.shape[-1]), (1, 0, 2))

            ws_small = gmlp_ws[j][:, :t_s, :t_s].astype(BF16).astype(F32)
            cat_s = _mix1_sample(ws_small, gmlp_b[j][:, :t_s], jnp.transpose(state_pool[j], (1, 0, 2)),
                                 steps_first(hp), steps_first(u), steps_first(v16), pw16, ps, t_s=t_s, past=past)
            cat = jnp.concatenate([cat_p, jnp.transpose(cat_s, (1, 0, 2)).reshape(b * t_s, -1)], axis=0)
            w_out16 = pool_gmlp_w_out[j].astype(BF16)

            hp_s = hp[seq:].reshape(b, t_s, POOL_WIDTH)
            outs.setdefault("pool_p", []).append(hp[seq - POOL_BUF:seq].reshape(bp, POOL_BUF, POOL_WIDTH))
            outs.setdefault("pool_s", []).append(jnp.concatenate([state_pool[j], hp_s], axis=1)[:, -POOL_BUF:])
            outs.setdefault("gv_s", []).append(v32[seq:].reshape(b, t_s, GMLP_WIDTH))

        x = _out_proj(cat, w_out16, g[1:2], x, tm=tm)
        x = _mlp(x, g[2:3], g[3:4], w_up[i].astype(BF16), w_down[i].astype(BF16), tm=tm_mlp, tf=tf)

    st = lambda key: jnp.stack(outs[key])
    return (x[:seq].reshape(bp, seq, d), x[seq:].reshape(b, t_s, d),
            st("ckv_p"), st("kpe_p"), st("k_p"), st("v_p"), st("pool_p"),
            st("ckv_s"), st("kpe_s"), st("k_s"), st("v_s"), st("pool_s"), st("gv_s"))
```

```python
import functools
import math

import jax
import jax.numpy as jnp
from jax import lax
from jax.experimental import pallas as pl
from jax.experimental.pallas import tpu as pltpu

F32 = jnp.float32
BF16 = jnp.bfloat16

EPS = 1e-6
ROPE_THETA = 10000.0
MLA_HEADS = 8
MLA_NOPE = 128
MLA_ROPE = 64
MLA_V = 128
Q_LORA = 512
KV_LORA = 512
MLA_SCALE = (MLA_NOPE + MLA_ROPE) ** -0.5
DIFF_HEADS = 8
DIFF_KV_HEADS = 2
DIFF_REP = DIFF_HEADS // DIFF_KV_HEADS
DIFF_HD = 64
DIFF_VD = 2 * DIFF_HD
DIFF_SCALE = DIFF_HD ** -0.5
POOL_WINDOWS = (2, 4, 8, 16)
POOL_GROUP = 256
POOL_WIDTH = len(POOL_WINDOWS) * POOL_GROUP
POOL_BUF = max(POOL_WINDOWS) - 1
CHUNK = 128
GMLP_GROUPS = 4
GMLP_GROUP = 256
GMLP_WIDTH = GMLP_GROUPS * GMLP_GROUP

LOG2E = 1.4426950408889634
NEG_BIG = -1e30
LANES = 128
V7X_VMEM_BUDGET = 56 * 1024 * 1024

_Z_CQ = 0
_Z_CKV = _Z_CQ + Q_LORA
_Z_DQ = _Z_CKV + KV_LORA
_Z_DK = _Z_DQ + DIFF_HEADS * 2 * DIFF_HD
_Z_DV = _Z_DK + DIFF_KV_HEADS * 2 * DIFF_HD
_Z_KPE = _Z_DV + DIFF_KV_HEADS * DIFF_VD
_Z_END = _Z_KPE + LANES


def _cparams(sem, vmem_bytes):
    return pltpu.CompilerParams(dimension_semantics=sem,
                                vmem_limit_bytes=int(min(vmem_bytes, V7X_VMEM_BUDGET)))


def _rms(x, g):
    return x * lax.rsqrt(jnp.mean(x * x, axis=-1, keepdims=True) + EPS) * g


def _dot(a, b):
    return jnp.dot(a, b, preferred_element_type=F32)


def _dot_nt(a, b):
    return lax.dot_general(a, b, (((1,), (1,)), ((), ())), preferred_element_type=F32)


def _rope128(x, cos, sin_signed, first_half):
    swapped = jnp.where(first_half, pltpu.roll(x, LANES - 32, axis=1), pltpu.roll(x, 32, axis=1))
    return x * cos + swapped * sin_signed


def _row_specs(tm, width, n_p):
    return [pl.BlockSpec((tm, width), lambda i, *_: (jnp.minimum(i, n_p - 1), 0)),
            pl.BlockSpec((tm, width), lambda i, *_: (jnp.maximum(i - n_p, 0), 0))]


def _read_rows(i, n_p, p_ref, s_ref):
    return jnp.where(i < n_p, p_ref[...], s_ref[...])


_QK_W = 2 * LANES


def _proj0_kernel(xp_ref, xs_ref, g_ref, win_ref, qn_ref, wuq_ref, kvn_ref, wuk_ref, wukf_ref, wuvf_ref, inv_ref,
                  qf_ref, kf_ref, vf_ref, qlat_ref, qpe_ref, ckv16_ref, kpe16_ref,
                  ckv32_ref, kpe32_ref, dq_ref, dk32_ref, dk16_ref, dv32_ref, dv16_ref,
                  *, tm, n_p, seq, past, t_s):
    i = pl.program_id(0)
    h = _rms(_read_rows(i, n_p, xp_ref, xs_ref), g_ref[...]).astype(BF16)
    z = _dot(h, win_ref[...])

    row = lax.broadcasted_iota(jnp.int32, (tm, LANES), 0) + i * tm
    pos = jnp.where(row < seq, row, past + lax.rem(row - seq, t_s)).astype(F32)
    ang = pos * inv_ref[...]
    cos = jnp.cos(ang)
    sin = jnp.sin(ang)
    lane = lax.broadcasted_iota(jnp.int32, (tm, LANES), 1)
    first_half = (lane & 32) == 0
    sin_signed = jnp.where(first_half, -sin, sin)

    ckv = _rms(z[:, _Z_CKV:_Z_CKV + KV_LORA], kvn_ref[...])
    ckv32_ref[...] = ckv
    ckv16 = ckv.astype(BF16)
    kpe = _rope128(z[:, _Z_KPE:_Z_KPE + LANES], cos, sin_signed, first_half)[:, :MLA_ROPE]
    kpe32_ref[...] = kpe
    kpe16 = kpe.astype(BF16)

    dq_ref[...] = (z[:, _Z_DQ:_Z_DK] * DIFF_SCALE).astype(BF16)
    dk = z[:, _Z_DK:_Z_DV]
    dk32_ref[...] = dk
    dk16_ref[...] = dk.astype(BF16)
    dv = z[:, _Z_DV:_Z_KPE]
    dv32_ref[...] = dv
    dv16_ref[...] = dv.astype(BF16)

    cq = _rms(z[:, _Z_CQ:_Z_CQ + Q_LORA], qn_ref[...]).astype(BF16)
    q = _dot(cq, wuq_ref[...])
    pe0 = MLA_HEADS * MLA_NOPE
    q_pe = [_rope128(q[:, pe0 + c * LANES:pe0 + (c + 1) * LANES], cos, sin_signed, first_half).astype(BF16)
            for c in range(MLA_HEADS * MLA_ROPE // LANES)]

    @pl.when(i < n_p)
    def _():
        kn = _dot(ckv16, wukf_ref[...])
        vf_ref[...] = _dot(ckv16, wuvf_ref[...]).astype(BF16)
        zero = jnp.zeros((tm, _QK_W - MLA_NOPE - MLA_ROPE), BF16)
        for hd in range(MLA_HEADS):
            cols = slice(hd * MLA_NOPE, (hd + 1) * MLA_NOPE)
            half = slice((hd % 2) * MLA_ROPE, (hd % 2 + 1) * MLA_ROPE)
            kf_ref[hd, :, :MLA_NOPE] = kn[:, cols].astype(BF16)
            kf_ref[hd, :, MLA_NOPE:MLA_NOPE + MLA_ROPE] = kpe16
            kf_ref[hd, :, MLA_NOPE + MLA_ROPE:] = zero
            qf_ref[hd, :, :MLA_NOPE] = q[:, cols].astype(BF16)
            qf_ref[hd, :, MLA_NOPE:MLA_NOPE + MLA_ROPE] = q_pe[hd // 2][:, half]
            qf_ref[hd, :, MLA_NOPE + MLA_ROPE:] = zero

    @pl.when(i >= n_p)
    def _():
        ckv16_ref[...] = ckv16
        kpe16_ref[...] = kpe16
        for hd in range(MLA_HEADS):
            qn = q[:, hd * MLA_NOPE:(hd + 1) * MLA_NOPE].astype(BF16)
            qlat_ref[hd] = _dot(qn, wuk_ref[hd]).astype(BF16)
            qpe_ref[hd] = q_pe[hd // 2][:, (hd % 2) * MLA_ROPE:(hd % 2 + 1) * MLA_ROPE]


def _proj0(xp, xs, g, win, qn, wuq, kvn, wuk_t, wuk_f, wuv_f, inv, *, past, t_s, tm):
    seq, d = xp.shape
    ns = xs.shape[0]
    n = seq + ns
    n_p = seq // tm
    const2 = lambda i: (0, 0)
    row2 = lambda i: (i, 0)
    prow3 = lambda i: (0, jnp.minimum(i, n_p - 1), 0)
    srow3 = lambda i: (0, jnp.maximum(i - n_p, 0), 0)
    prow2 = lambda i: (jnp.minimum(i, n_p - 1), 0)
    srow2 = lambda i: (jnp.maximum(i - n_p, 0), 0)
    kw = DIFF_KV_HEADS * 2 * DIFF_HD
    vw = DIFF_KV_HEADS * DIFF_VD
    out_shape = (
        jax.ShapeDtypeStruct((MLA_HEADS, seq, _QK_W), BF16),
        jax.ShapeDtypeStruct((MLA_HEADS, seq, _QK_W), BF16),
        jax.ShapeDtypeStruct((seq, MLA_HEADS * MLA_V), BF16),
        jax.ShapeDtypeStruct((MLA_HEADS, ns, KV_LORA), BF16),
        jax.ShapeDtypeStruct((MLA_HEADS, ns, MLA_ROPE), BF16),
        jax.ShapeDtypeStruct((ns, KV_LORA), BF16),
        jax.ShapeDtypeStruct((ns, MLA_ROPE), BF16),
        jax.ShapeDtypeStruct((n, KV_LORA), F32),
        jax.ShapeDtypeStruct((n, MLA_ROPE), F32),
        jax.ShapeDtypeStruct((n, DIFF_HEADS * 2 * DIFF_HD), BF16),
        jax.ShapeDtypeStruct((n, kw), F32),
        jax.ShapeDtypeStruct((n, kw), BF16),
        jax.ShapeDtypeStruct((n, vw), F32),
        jax.ShapeDtypeStruct((n, vw), BF16),
    )
    out_specs = (
        pl.BlockSpec((MLA_HEADS, tm, _QK_W), prow3),
        pl.BlockSpec((MLA_HEADS, tm, _QK_W), prow3),
        pl.BlockSpec((tm, MLA_HEADS * MLA_V), prow2),
        pl.BlockSpec((MLA_HEADS, tm, KV_LORA), srow3),
        pl.BlockSpec((MLA_HEADS, tm, MLA_ROPE), srow3),
        pl.BlockSpec((tm, KV_LORA), srow2),
        pl.BlockSpec((tm, MLA_ROPE), srow2),
        pl.BlockSpec((tm, KV_LORA), row2),
        pl.BlockSpec((tm, MLA_ROPE), row2),
        pl.BlockSpec((tm, DIFF_HEADS * 2 * DIFF_HD), row2),
        pl.BlockSpec((tm, kw), row2),
        pl.BlockSpec((tm, kw), row2),
        pl.BlockSpec((tm, vw), row2),
        pl.BlockSpec((tm, vw), row2),
    )
    in_specs = _row_specs(tm, d, n_p) + [
        pl.BlockSpec((1, d), const2),
        pl.BlockSpec(win.shape, const2),
        pl.BlockSpec((1, Q_LORA), const2),
        pl.BlockSpec(wuq.shape, const2),
        pl.BlockSpec((1, KV_LORA), const2),
        pl.BlockSpec(wuk_t.shape, lambda i: (0, 0, 0)),
        pl.BlockSpec(wuk_f.shape, const2),
        pl.BlockSpec(wuv_f.shape, const2),
        pl.BlockSpec((1, LANES), const2),
    ]
    weights = win.size + wuq.size + wuk_t.size + wuk_f.size + wuv_f.size
    vmem = 2 * weights * 2 + 4 * tm * d * 4 + 16 * tm * _Z_END * 4
    return pl.pallas_call(
        functools.partial(_proj0_kernel, tm=tm, n_p=n_p, seq=seq, past=past, t_s=t_s),
        grid=(n // tm,), in_specs=in_specs, out_specs=out_specs, out_shape=out_shape,
        compiler_params=_cparams(("arbitrary",), vmem), name="proj0",
    )(xp, xs, g, win, qn, wuq, kvn, wuk_t, wuk_f, wuv_f, inv)


def _lanes_to(x, width):
    if width <= LANES:
        return x[:, :width]
    return jnp.concatenate([x] * (width // LANES), axis=1)


def _softmax_update(s2, v16, m_ref, l_ref, acc_ref):
    m_prev = m_ref[...]
    m_new = jnp.maximum(m_prev, jnp.max(s2, axis=-1, keepdims=True))
    p = jnp.exp2(s2 - _lanes_to(m_new, s2.shape[1]))
    alpha = jnp.exp2(m_prev - m_new)
    l_ref[...] = alpha * l_ref[...] + jnp.sum(p, axis=-1, keepdims=True)
    acc_ref[...] = _lanes_to(alpha, acc_ref.shape[-1]) * acc_ref[...] + _dot(p.astype(BF16), v16)
    m_ref[...] = m_new


def _softmax_finish(acc_ref, l_ref):
    return acc_ref[...] / _lanes_to(l_ref[...], acc_ref.shape[-1])


def _diff_lambda(lam_ref, lam_init):
    lp = lam_ref[...]
    a = jnp.sum(lp[0:1] * lp[1:2], axis=-1, keepdims=True)
    b = jnp.sum(lp[2:3] * lp[3:4], axis=-1, keepdims=True)
    return jnp.exp(a) - jnp.exp(b) + lam_init


_HEADS_PER_STEP = 4


def _mla_prompt_kernel(q_ref, k_ref, v_ref, o_ref, m_ref, l_ref, acc_ref, *, tq, tk):
    i = pl.program_id(1)
    m_ref[...] = jnp.full(m_ref.shape, NEG_BIG, F32)
    l_ref[...] = jnp.zeros(l_ref.shape, F32)
    acc_ref[...] = jnp.zeros(acc_ref.shape, F32)
    c = MLA_SCALE * LOG2E

    def block(j, masked):
        k0 = pl.multiple_of(j * tk, tk)
        s2s = [_dot_nt(q_ref[hh], k_ref[hh, pl.ds(k0, tk), :]) * c for hh in range(_HEADS_PER_STEP)]
        if masked:
            qpos = i * tq + lax.broadcasted_iota(jnp.int32, (tq, tk), 0)
            kpos = j * tk + lax.broadcasted_iota(jnp.int32, (tq, tk), 1)
            s2s = [jnp.where(kpos <= qpos, s2, NEG_BIG) for s2 in s2s]
        for hh, s2 in enumerate(s2s):
            _softmax_update(s2, v_ref[pl.ds(k0, tk), hh * MLA_V:(hh + 1) * MLA_V],
                            m_ref.at[hh], l_ref.at[hh], acc_ref.at[hh])

    def full_block(j, carry):
        block(j, False)
        return carry

    jd = (i * tq) // tk
    lax.fori_loop(0, jd, full_block, 0)
    block(jd, True)
    for hh in range(_HEADS_PER_STEP):
        o_ref[:, hh * MLA_V:(hh + 1) * MLA_V] = _softmax_finish(acc_ref.at[hh], l_ref.at[hh]).astype(BF16)


def _mla_prompt(qf, kf, vf, *, tq, tk):
    heads, seq, w = qf.shape
    hps = _HEADS_PER_STEP
    assert heads % hps == 0 and tk % tq == 0
    vmem = (hps * seq * (w + MLA_V) * 2 + 4 * hps * tq * (w + MLA_V) * 2
            + hps * tq * (2 * LANES + MLA_V) * 4 + 6 * hps * tq * tk * 4)
    return pl.pallas_call(
        functools.partial(_mla_prompt_kernel, tq=tq, tk=tk),
        grid=(heads // hps, seq // tq),
        in_specs=[
            pl.BlockSpec((hps, tq, w), lambda hp, i: (hp, i, 0)),
            pl.BlockSpec((hps, seq, w), lambda hp, i: (hp, 0, 0), pipeline_mode=pl.Buffered(1)),
            pl.BlockSpec((seq, hps * MLA_V), lambda hp, i: (0, hp), pipeline_mode=pl.Buffered(1)),
        ],
        out_specs=pl.BlockSpec((tq, hps * MLA_V), lambda hp, i: (i, hp)),
        out_shape=jax.ShapeDtypeStruct((seq, heads * MLA_V), BF16),
        scratch_shapes=[pltpu.VMEM((hps, tq, LANES), F32), pltpu.VMEM((hps, tq, LANES), F32),
                        pltpu.VMEM((hps, tq, MLA_V), F32)],
        compiler_params=_cparams(("parallel", "parallel"), vmem), name="mla_prompt",
    )(qf, kf, vf)


def _diff_finish(a, lam, sub, scale_out):
    half = a.shape[0] // 2
    o = a[:half] - lam * a[half:]
    return _rms(o, sub) * scale_out


def _diff_prompt_kernel(dq_ref, dk_ref, dv_ref, lam_ref, sub_ref, o_ref,
                        q_scr, m_ref, l_ref, acc_ref, *, tq, tk, lam_init):
    i = pl.program_id(0)
    rows = 2 * DIFF_REP * tq
    lane = lax.broadcasted_iota(jnp.int32, (tq, 2 * DIFF_HD), 1)
    for g in range(DIFF_KV_HEADS):
        for r in range(DIFF_REP):
            hd = g * DIFF_REP + r
            qh = dq_ref[:, hd * 2 * DIFF_HD:(hd + 1) * 2 * DIFF_HD]
            zero = jnp.zeros_like(qh)
            q_scr[g, r * tq:(r + 1) * tq, :] = jnp.where(lane < DIFF_HD, qh, zero)
            q_scr[g, (DIFF_REP + r) * tq:(DIFF_REP + r + 1) * tq, :] = jnp.where(lane < DIFF_HD, zero, qh)
    m_ref[...] = jnp.full(m_ref.shape, NEG_BIG, F32)
    l_ref[...] = jnp.zeros(l_ref.shape, F32)
    acc_ref[...] = jnp.zeros(acc_ref.shape, F32)

    def block(j, masked):
        k0 = pl.multiple_of(j * tk, tk)
        s2s = [_dot_nt(q_scr[g], dk_ref[pl.ds(k0, tk), g * 2 * DIFF_HD:(g + 1) * 2 * DIFF_HD]) * LOG2E
               for g in range(DIFF_KV_HEADS)]
        if masked:
            qpos = i * tq + (lax.broadcasted_iota(jnp.int32, (rows, tk), 0) & (tq - 1))
            kpos = j * tk + lax.broadcasted_iota(jnp.int32, (rows, tk), 1)
            s2s = [jnp.where(kpos <= qpos, s2, NEG_BIG) for s2 in s2s]
        for g, s2 in enumerate(s2s):
            _softmax_update(s2, dv_ref[pl.ds(k0, tk), g * DIFF_VD:(g + 1) * DIFF_VD],
                            m_ref.at[g], l_ref.at[g], acc_ref.at[g])

    def full_block(j, carry):
        block(j, False)
        return carry

    jd = (i * tq) // tk
    lax.fori_loop(0, jd, full_block, 0)
    block(jd, True)

    lam = _diff_lambda(lam_ref, lam_init)
    for g in range(DIFF_KV_HEADS):
        o = _diff_finish(_softmax_finish(acc_ref.at[g], l_ref.at[g]), lam, sub_ref[...], 1.0 - lam_init)
        for r in range(DIFF_REP):
            hd = g * DIFF_REP + r
            o_ref[:, hd * DIFF_VD:(hd + 1) * DIFF_VD] = o[r * tq:(r + 1) * tq].astype(BF16)


def _diff_prompt(dq16, dk16, dv16, lam_p, sub, *, seq, tq, tk, lam_init):
    rows = 2 * DIFF_REP * tq
    kw = DIFF_KV_HEADS * 2 * DIFF_HD
    vmem = (2 * 2 * seq * kw * 2 + 4 * tq * DIFF_HEADS * 2 * DIFF_HD * 2
            + DIFF_KV_HEADS * rows * (2 * DIFF_HD * 2 + DIFF_VD * 4 + 2 * LANES * 4) + 6 * rows * tk * 4)
    return pl.pallas_call(
        functools.partial(_diff_prompt_kernel, tq=tq, tk=tk, lam_init=lam_init),
        grid=(seq // tq,),
        in_specs=[
            pl.BlockSpec((tq, DIFF_HEADS * 2 * DIFF_HD), lambda i: (i, 0)),
            pl.BlockSpec((seq, kw), lambda i: (0, 0)),
            pl.BlockSpec((seq, DIFF_KV_HEADS * DIFF_VD), lambda i: (0, 0)),
            pl.BlockSpec(lam_p.shape, lambda i: (0, 0)),
            pl.BlockSpec(sub.shape, lambda i: (0, 0)),
        ],
        out_specs=pl.BlockSpec((tq, DIFF_HEADS * DIFF_VD), lambda i: (i, 0)),
        out_shape=jax.ShapeDtypeStruct((seq, DIFF_HEADS * DIFF_VD), BF16),
        scratch_shapes=[pltpu.VMEM((DIFF_KV_HEADS, rows, 2 * DIFF_HD), BF16),
                        pltpu.VMEM((DIFF_KV_HEADS, rows, LANES), F32),
                        pltpu.VMEM((DIFF_KV_HEADS, rows, LANES), F32),
                        pltpu.VMEM((DIFF_KV_HEADS, rows, DIFF_VD), F32)],
        compiler_params=_cparams(("parallel",), vmem), name="diff_prompt",
    )(dq16, dk16, dv16, lam_p, sub)


_N_CACHES = 4


def _sample_attn_kernel(pt_ref, qlat_ref, qpe_ref, dq_ref, ckvn_ref, kpen_ref, dkn_ref, dvn_ref,
                        lam_ref, sub_ref, ckv_hbm, kpe_hbm, dk_hbm, dv_hbm, olat_ref, od_ref,
                        ckv_buf, kpe_buf, dk_buf, dv_buf, sems,
                        m1_ref, l1_ref, acc1_ref, m2_ref, l2_ref, acc2_ref, *, pages, layer, t_s, lam_init):
    steps = pl.num_programs(1)
    s = pl.program_id(1)
    n = pl.program_id(0) * steps + s
    total = pl.num_programs(0) * steps
    slot = lax.rem(n, 2)

    def page_copies(step, slot_):
        cps = []
        for kk in range(pages):
            pid = pt_ref[step * pages + kk]
            for ci, (hbm, buf) in enumerate(((ckv_hbm, ckv_buf), (kpe_hbm, kpe_buf),
                                             (dk_hbm, dk_buf), (dv_hbm, dv_buf))):
                cps.append(pltpu.make_async_copy(hbm.at[layer, pid], buf.at[slot_, kk], sems.at[slot_, ci]))
        return cps

    @pl.when(n == 0)
    def _():
        for cp in page_copies(0, 0):
            cp.start()

    @pl.when(n + 1 < total)
    def _():
        for cp in page_copies(n + 1, 1 - slot):
            cp.start()

    ql = qlat_ref[...]
    qp = qpe_ref[...]
    dq = dq_ref[...]
    c = MLA_SCALE * LOG2E
    rows1 = ql.shape[0]
    rows2 = dq.shape[0]
    grows = rows2 // DIFF_KV_HEADS

    def group_mask(keys):
        grp = lax.broadcasted_iota(jnp.int32, (rows2, keys), 0) // grows
        return (lax.broadcasted_iota(jnp.int32, (rows2, keys), 1) & (DIFF_KV_HEADS - 1)) == grp

    @pl.when(s == 0)
    def _():
        m1_ref[...] = jnp.full(m1_ref.shape, NEG_BIG, F32)
        l1_ref[...] = jnp.zeros(l1_ref.shape, F32)
        acc1_ref[...] = jnp.zeros(acc1_ref.shape, F32)
        m2_ref[...] = jnp.full(m2_ref.shape, NEG_BIG, F32)
        l2_ref[...] = jnp.zeros(l2_ref.shape, F32)
        acc2_ref[...] = jnp.zeros(acc2_ref.shape, F32)
        kn = ckvn_ref[...]
        npad = kn.shape[0]
        s2 = (_dot_nt(ql, kn) + _dot_nt(qp, kpen_ref[...])) * c
        step1 = lax.rem(lax.broadcasted_iota(jnp.int32, (rows1, npad), 0), t_s)
        key1 = lax.broadcasted_iota(jnp.int32, (rows1, npad), 1)
        _softmax_update(jnp.where(key1 <= step1, s2, NEG_BIG), kn, m1_ref, l1_ref, acc1_ref)
        dkn = dkn_ref[...]
        nk = dkn.shape[0]
        sd = _dot_nt(dq, dkn) * LOG2E
        step2 = lax.rem(lax.broadcasted_iota(jnp.int32, (rows2, nk), 0), t_s)
        key2 = lax.broadcasted_iota(jnp.int32, (rows2, nk), 1) // DIFF_KV_HEADS
        ok = jnp.logical_and(group_mask(nk), key2 <= step2)
        _softmax_update(jnp.where(ok, sd, NEG_BIG), dvn_ref[...], m2_ref, l2_ref, acc2_ref)

    for cp in page_copies(n, slot):
        cp.wait()

    k = jnp.concatenate([ckv_buf[slot, kk].astype(BF16) for kk in range(pages)], axis=0)
    kp_t = jnp.concatenate([kpe_buf[slot, kk].astype(BF16) for kk in range(pages)], axis=1)
    dk = jnp.concatenate([dk_buf[slot, kk].astype(BF16) for kk in range(pages)], axis=0)
    dv = jnp.concatenate([dv_buf[slot, kk].astype(BF16) for kk in range(pages)], axis=0)
    s2 = (_dot_nt(ql, k) + _dot(qp, kp_t)) * c
    sd = jnp.where(group_mask(dk.shape[0]), _dot_nt(dq, dk) * LOG2E, NEG_BIG)
    _softmax_update(s2, k, m1_ref, l1_ref, acc1_ref)
    _softmax_update(sd, dv, m2_ref, l2_ref, acc2_ref)

    @pl.when(s == steps - 1)
    def _():
        olat_ref[...] = _softmax_finish(acc1_ref, l1_ref).astype(BF16)
        lam = _diff_lambda(lam_ref, lam_init)
        a = _softmax_finish(acc2_ref, l2_ref)
        for g in range(DIFF_KV_HEADS):
            od_ref[g] = _diff_finish(a[g * grows:(g + 1) * grows], lam, sub_ref[...], 1.0 - lam_init).astype(BF16)


def _sample_attn(page_table, qlat_s, qpe_s, dq_s, ckvn, kpen, dkn, dvn, lam_p, sub,
                 cache_ckv, cache_kpe, cache_k, cache_v, *, layer, pages, t_s, lam_init):
    b, n_pages = page_table.shape
    assert n_pages % pages == 0
    steps = n_pages // pages
    rows1 = qlat_s.shape[1]
    rows2 = dq_s.shape[1]
    grows = rows2 // DIFF_KV_HEADS

    def seq3(a):
        return pl.BlockSpec((None,) + a.shape[1:], lambda bi, si, pt: (bi, 0, 0))

    hbm = pl.BlockSpec(memory_space=pl.ANY)
    in_specs = [
        seq3(qlat_s), seq3(qpe_s), seq3(dq_s), seq3(ckvn), seq3(kpen), seq3(dkn), seq3(dvn),
        pl.BlockSpec(lam_p.shape, lambda bi, si, pt: (0, 0)),
        pl.BlockSpec(sub.shape, lambda bi, si, pt: (0, 0)),
        hbm, hbm, hbm, hbm,
    ]
    out_specs = (
        pl.BlockSpec((None, rows1, KV_LORA), lambda bi, si, pt: (bi, 0, 0)),
        pl.BlockSpec((None, DIFF_KV_HEADS, grows // 2, DIFF_VD), lambda bi, si, pt: (bi, 0, 0, 0)),
    )
    out_shape = (
        jax.ShapeDtypeStruct((b, rows1, KV_LORA), BF16),
        jax.ShapeDtypeStruct((b, DIFF_KV_HEADS, grows // 2, DIFF_VD), BF16),
    )
    bufs = [pltpu.VMEM((2, pages) + c.shape[2:], c.dtype) for c in (cache_ckv, cache_kpe, cache_k, cache_v)]
    scratch = bufs + [
        pltpu.SemaphoreType.DMA((2, _N_CACHES)),
        pltpu.VMEM((rows1, LANES), F32), pltpu.VMEM((rows1, LANES), F32), pltpu.VMEM((rows1, KV_LORA), F32),
        pltpu.VMEM((rows2, LANES), F32), pltpu.VMEM((rows2, LANES), F32), pltpu.VMEM((rows2, DIFF_VD), F32)]
    step_bytes = pages * sum(math.prod(c.shape[2:]) for c in (cache_ckv, cache_kpe, cache_k, cache_v)) * 4
    vmem = 2 * step_bytes + 2 * step_bytes + 6 * 1024 * 1024
    grid_spec = pltpu.PrefetchScalarGridSpec(
        num_scalar_prefetch=1, grid=(b, steps), in_specs=in_specs, out_specs=out_specs,
        scratch_shapes=scratch)
    return pl.pallas_call(
        functools.partial(_sample_attn_kernel, pages=pages, layer=layer, t_s=t_s, lam_init=lam_init),
        grid_spec=grid_spec, out_shape=out_shape,
        compiler_params=_cparams(("arbitrary", "arbitrary"), vmem), name="sample_attn",
    )(page_table.reshape(-1), qlat_s, qpe_s, dq_s, ckvn, kpen, dkn, dvn, lam_p, sub,
      cache_ckv, cache_kpe, cache_k, cache_v)


def _head_proj_kernel(o_ref, w_ref, out_ref):
    out_ref[...] = _dot(o_ref[...], w_ref[...]).astype(BF16)


def _head_proj(o_t, wuv_t):
    heads, rows, r = o_t.shape
    return pl.pallas_call(
        _head_proj_kernel, grid=(heads,),
        in_specs=[pl.BlockSpec((None, rows, r), lambda hd: (hd, 0, 0)),
                  pl.BlockSpec((None, r, MLA_V), lambda hd: (hd, 0, 0))],
        out_specs=pl.BlockSpec((rows, MLA_V), lambda hd: (0, hd)),
        out_shape=jax.ShapeDtypeStruct((rows, heads * MLA_V), BF16),
        compiler_params=_cparams(("parallel",), 16 * 1024 * 1024), name="head_proj",
    )(o_t, wuv_t)


def _out_proj_kernel(ap_ref, as_ref, bp_ref, bs_ref, w_ref, g_ref, *rest, n_p):
    *x_refs, o_ref = rest
    i = pl.program_id(0)
    ka = ap_ref.shape[1]
    mix = (_dot(_read_rows(i, n_p, ap_ref, as_ref), w_ref[:ka, :])
           + _dot(_read_rows(i, n_p, bp_ref, bs_ref), w_ref[ka:, :]))
    x = x_refs[0][...] if len(x_refs) == 1 else _read_rows(i, n_p, *x_refs)
    o_ref[...] = x + _rms(mix, g_ref[...])


def _out_proj(a_p, a_s, b_p, b_s, w16, g, xs, *, tm):
    seq, ka = a_p.shape
    kb = b_p.shape[1]
    n = seq + a_s.shape[0]
    d = w16.shape[1]
    n_p = seq // tm
    x_specs = [pl.BlockSpec((tm, d), lambda i: (i, 0))] if len(xs) == 1 else _row_specs(tm, d, n_p)
    vmem = 2 * (ka + kb) * d * 2 + 4 * tm * (ka + kb) * 2 + 12 * tm * d * 4
    return pl.pallas_call(
        functools.partial(_out_proj_kernel, n_p=n_p), grid=(n // tm,),
        in_specs=(_row_specs(tm, ka, n_p) + _row_specs(tm, kb, n_p)
                  + [pl.BlockSpec((ka + kb, d), lambda i: (0, 0)), pl.BlockSpec((1, d), lambda i: (0, 0))]
                  + x_specs),
        out_specs=pl.BlockSpec((tm, d), lambda i: (i, 0)),
        out_shape=jax.ShapeDtypeStruct((n, d), F32),
        compiler_params=_cparams(("parallel",), vmem), name="out_proj",
    )(a_p, a_s, b_p, b_s, w16, g, *xs)


def _mlp_kernel(x_ref, gpre_ref, gpost_ref, wup_ref, wdn_ref, o_ref, h_scr, acc_scr):
    k = pl.program_id(1)

    @pl.when(k == 0)
    def _():
        h_scr[...] = _rms(x_ref[...], gpre_ref[...]).astype(BF16)
        acc_scr[...] = jnp.zeros(acc_scr.shape, F32)

    a = jnp.maximum(_dot(h_scr[...], wup_ref[...]), 0.0)
    acc_scr[...] += _dot((a * a).astype(BF16), wdn_ref[...])

    @pl.when(k == pl.num_programs(1) - 1)
    def _():
        o_ref[...] = x_ref[...] + _rms(acc_scr[...], gpost_ref[...])


def _mlp(x, gpre, gpost, wup16, wdn16, *, layer, tm, tf):
    n, d = x.shape
    f = wup16.shape[2]
    vmem = 2 * 2 * d * tf * 2 + 4 * tm * d * 4 + tm * d * 2 + tm * d * 4 + 3 * tm * tf * 4 + 2 * tm * d * 4
    return pl.pallas_call(
        _mlp_kernel, grid=(n // tm, f // tf),
        in_specs=[pl.BlockSpec((tm, d), lambda i, k: (i, 0)),
                  pl.BlockSpec((1, d), lambda i, k: (0, 0)),
                  pl.BlockSpec((1, d), lambda i, k: (0, 0)),
                  pl.BlockSpec((None, d, tf), lambda i, k: (layer, 0, k)),
                  pl.BlockSpec((None, tf, d), lambda i, k: (layer, k, 0))],
        out_specs=pl.BlockSpec((tm, d), lambda i, k: (i, 0)),
        out_shape=jax.ShapeDtypeStruct((n, d), F32),
        scratch_shapes=[pltpu.VMEM((tm, d), BF16), pltpu.VMEM((tm, d), F32)],
        compiler_params=_cparams(("parallel", "arbitrary"), vmem), name="mlp",
    )(x, gpre, gpost, wup16, wdn16)


def _proj1_kernel(x_ref, g_ref, win_ref, gn_ref, hp_ref, u_ref, v32_ref, v16_ref):
    h = _rms(x_ref[...], g_ref[...]).astype(BF16)
    z = _dot(h, win_ref[...])
    hp_ref[...] = z[:, :POOL_WIDTH]
    uv = jax.nn.gelu(z[:, POOL_WIDTH:])
    u_ref[...] = uv[:, :GMLP_WIDTH]
    vv = uv[:, GMLP_WIDTH:]
    xc = vv - jnp.mean(vv, axis=-1, keepdims=True)
    v = xc * lax.rsqrt(jnp.mean(xc * xc, axis=-1, keepdims=True) + EPS) * gn_ref[...]
    v32_ref[...] = v
    v16_ref[...] = v.astype(BF16)


def _proj1(x, g, win16, gn, *, tm):
    n, d = x.shape
    w = win16.shape[1]
    row2 = lambda i: (i, 0)
    const2 = lambda i: (0, 0)
    vmem = 2 * win16.size * 2 + 2 * tm * d * 4 + 10 * tm * w * 4
    return pl.pallas_call(
        _proj1_kernel, grid=(n // tm,),
        in_specs=[pl.BlockSpec((tm, d), row2), pl.BlockSpec((1, d), const2),
                  pl.BlockSpec(win16.shape, const2), pl.BlockSpec((1, GMLP_WIDTH), const2)],
        out_specs=(pl.BlockSpec((tm, POOL_WIDTH), row2), pl.BlockSpec((tm, GMLP_WIDTH), row2),
                   pl.BlockSpec((tm, GMLP_WIDTH), row2), pl.BlockSpec((tm, GMLP_WIDTH), row2)),
        out_shape=(jax.ShapeDtypeStruct((n, POOL_WIDTH), F32), jax.ShapeDtypeStruct((n, GMLP_WIDTH), F32),
                   jax.ShapeDtypeStruct((n, GMLP_WIDTH), F32), jax.ShapeDtypeStruct((n, GMLP_WIDTH), BF16)),
        compiler_params=_cparams(("parallel",), vmem), name="proj1",
    )(x, g, win16, gn)


_HALO = 16


def _mix1_prompt_kernel(hp_ref, halo_ref, u_ref, v_ref, pw_ref, ps_ref, ws_ref, bt_ref, po_ref, go_ref, *, tm):
    i = pl.program_id(0)
    halo = jnp.where(i == 0, 0.0, halo_ref[...])
    ext = jnp.concatenate([halo, hp_ref[...]], axis=0)
    pos = i * tm + lax.broadcasted_iota(jnp.int32, (tm, POOL_GROUP), 0)
    for gi, w in enumerate(POOL_WINDOWS):
        run = ext[:, gi * POOL_GROUP:(gi + 1) * POOL_GROUP]
        width = 1
        while width < w:
            run = run + pltpu.roll(run, width, axis=0)
            width *= 2
        wsum = run[_HALO:]
        cnt = jnp.minimum(pos + 1, w).astype(F32)
        d = (wsum / cnt - hp_ref[:, gi * POOL_GROUP:(gi + 1) * POOL_GROUP]).astype(BF16)
        po = _dot(d, pw_ref[gi]) * ps_ref[:, gi * POOL_GROUP:(gi + 1) * POOL_GROUP]
        po_ref[:, gi * POOL_GROUP:(gi + 1) * POOL_GROUP] = po.astype(BF16)

    tril =(lax.broadcasted_iota(jnp.int32, (CHUNK, CHUNK), 1)
            <= lax.broadcasted_iota(jnp.int32, (CHUNK, CHUNK), 0))
    for g in range(GMLP_GROUPS):
        wm = jnp.where(tril, ws_ref[g], 0.0).astype(BF16)
        bias = bt_ref[:, g:g + 1]
        for ch in range(tm // CHUNK):
            rows = slice(ch * CHUNK, (ch + 1) * CHUNK)
            cols = slice(g * GMLP_GROUP, (g + 1) * GMLP_GROUP)
            mixed = _dot(wm, v_ref[rows, cols]) + bias
            go_ref[rows, cols] = (u_ref[rows, cols] * mixed).astype(BF16)


def _mix1_prompt(hp, u, v16, pw16, ps, ws, b_t, *, seq, tm):
    halo_blocks = tm // _HALO
    row2 = lambda i: (i, 0)
    vmem = 24 * tm * POOL_WIDTH * 4 + 4 * 1024 * 1024
    return pl.pallas_call(
        functools.partial(_mix1_prompt_kernel, tm=tm), grid=(seq // tm,),
        in_specs=[pl.BlockSpec((tm, POOL_WIDTH), row2),
                  pl.BlockSpec((_HALO, POOL_WIDTH), lambda i: (jnp.maximum(i * halo_blocks - 1, 0), 0)),
                  pl.BlockSpec((tm, GMLP_WIDTH), row2),
                  pl.BlockSpec((tm, GMLP_WIDTH), row2),
                  pl.BlockSpec(pw16.shape, lambda i: (0, 0, 0)),
                  pl.BlockSpec(ps.shape, lambda i: (0, 0)),
                  pl.BlockSpec(ws.shape, lambda i: (0, 0, 0)),
                  pl.BlockSpec(b_t.shape, lambda i: (0, 0))],
        out_specs=(pl.BlockSpec((tm, POOL_WIDTH), row2), pl.BlockSpec((tm, GMLP_WIDTH), row2)),
        out_shape=(jax.ShapeDtypeStruct((seq, POOL_WIDTH), BF16), jax.ShapeDtypeStruct((seq, GMLP_WIDTH), BF16)),
        compiler_params=_cparams(("parallel",), vmem), name="mix1_prompt",
    )(hp, hp, u, v16, pw16, ps, ws, b_t)


def _mix1_sample_kernel(ws_ref, b_ref, st_ref, hp_ref, u_ref, v_ref, pw_ref, ps_ref, po_ref, go_ref, *, t_s, past):
    for gi, w in enumerate(POOL_WINDOWS):
        cols = slice(gi * POOL_GROUP, (gi + 1) * POOL_GROUP)
        tails = [None] * w
        tail = None
        for k in range(1, w):
            row = st_ref[POOL_BUF - k, :, cols]
            tail = row if tail is None else tail + row
            tails[k] = tail
        for t in range(t_s):
            wsum = None
            for tp in range(max(0, t - w + 1), t + 1):
                r = hp_ref[tp, :, cols]
                wsum = r if wsum is None else wsum + r
            nbuf = max(w - 1 - t, 0)
            if nbuf > 0:
                wsum = wsum + tails[nbuf]
            cnt = float(min(past + t + 1, w))
            d = (wsum / cnt - hp_ref[t, :, cols]).astype(BF16)
            po_ref[t, :, cols] = (_dot(d, pw_ref[gi]) * ps_ref[:, cols]).astype(BF16)
    for g in range(GMLP_GROUPS):
        cols = slice(g * GMLP_GROUP, (g + 1) * GMLP_GROUP)
        for t in range(t_s):
            mixed = None
            for j in range(t + 1):
                term = ws_ref[g, t, j] * v_ref[j, :, cols].astype(F32)
                mixed = term if mixed is None else mixed + term
            mixed = mixed + b_ref[g, t]
            go_ref[t, :, cols] = (u_ref[t, :, cols] * mixed).astype(BF16)


def _mix1_sample(ws_small, b_small, st_t, hp_t, u_t, v16_t, pw16, ps, *, t_s, past):
    b = hp_t.shape[1]
    full = lambda a: pl.BlockSpec(a.shape, lambda i, n=a.ndim: (0,) * n)
    smem = lambda a: pl.BlockSpec(a.shape, lambda i, n=a.ndim: (0,) * n, memory_space=pltpu.SMEM)
    args = (ws_small, b_small, st_t, hp_t, u_t, v16_t, pw16, ps)
    in_specs = [smem(ws_small), smem(b_small)] + [full(a) for a in args[2:]]
    vmem = 4 * (st_t.size * 4 + 3 * hp_t.size * 4) + 8 * 1024 * 1024
    return pl.pallas_call(
        functools.partial(_mix1_sample_kernel, t_s=t_s, past=past), grid=(1,),
        in_specs=in_specs,
        out_specs=(pl.BlockSpec((t_s, b, POOL_WIDTH), lambda i: (0, 0, 0)),
                   pl.BlockSpec((t_s, b, GMLP_WIDTH), lambda i: (0, 0, 0))),
        out_shape=(jax.ShapeDtypeStruct((t_s, b, POOL_WIDTH), BF16),
                   jax.ShapeDtypeStruct((t_s, b, GMLP_WIDTH), BF16)),
        compiler_params=_cparams(("arbitrary",), vmem), name="mix1_sample",
    )(*args)


def _pick_tile(n, prefs):
    for t in prefs:
        if n % t == 0:
            return t
    raise ValueError(f"no tile in {prefs} divides {n}")


def kernel(x_prompt, x_sample, cache_mla_ckv, cache_mla_kpe, cache_diff_k, cache_diff_v, state_pool, page_table, norm_gains, w_up, w_down, mla_diff_w_in, mla_q_norm, mla_w_uq, mla_kv_norm, mla_w_uk, mla_w_uv, diff_lambda, diff_subln, mla_diff_w_out, pool_gmlp_w_in, pool_w, pool_scale, gmlp_norm, gmlp_ws, gmlp_b, pool_gmlp_w_out):
    bp, seq, d = x_prompt.shape
    b, t_s, _ = x_sample.shape
    assert bp == 1, "one prompt sequence"
    n_pages = page_table.shape[1]
    page = cache_mla_ckv.shape[2]
    past = n_pages * page
    n = seq + b * t_s
    depth = norm_gains.shape[0]
    assert t_s <= 16 and (MLA_HEADS * t_s) % 16 == 0

    ns = b * t_s
    tm = _pick_tile(math.gcd(seq, ns), (256, 128))
    tm_mlp = _pick_tile(n, (512, 256, 128))
    tf = _pick_tile(w_up.shape[2], (1024, 512))
    tq_d = _pick_tile(seq, (128,))
    tk_d = _pick_tile(seq, (512, 256, 128))
    tq_m = _pick_tile(seq, (512, 256, 128))
    pages = _pick_tile(n_pages, (16, 8, 4, 2, 1))

    xs = (x_prompt.reshape(seq, d), x_sample.reshape(ns, d))
    wup16 = w_up.astype(BF16)
    wdn16 = w_down.astype(BF16)
    half = MLA_ROPE // 2
    inv = ROPE_THETA ** (-jnp.arange(half, dtype=F32) / half)
    inv_tab = jnp.tile(inv, LANES // half).reshape(1, LANES)

    outs = {}
    for i in range(depth):
        g = norm_gains[i]
        j = i // 2
        if i % 2 == 0:
            lam_init = 0.8 - 0.6 * math.exp(-0.3 * i)
            w_in = mla_diff_w_in[j]
            o_cq, o_ckv, o_kpe = 0, Q_LORA, Q_LORA + KV_LORA
            o_dq = o_kpe + MLA_ROPE
            o_dk = o_dq + DIFF_HEADS * 2 * DIFF_HD
            o_dv = o_dk + DIFF_KV_HEADS * 2 * DIFF_HD
            win16 = jnp.concatenate(
                [w_in[:, o_cq:o_ckv], w_in[:, o_ckv:o_kpe], w_in[:, o_dq:], w_in[:, o_kpe:o_dq],
                 jnp.zeros((d, LANES - MLA_ROPE), F32)], axis=1).astype(BF16)
            wuq = mla_w_uq[j].reshape(Q_LORA, MLA_HEADS, MLA_NOPE + MLA_ROPE)
            wuq16 = jnp.concatenate([wuq[:, :, :MLA_NOPE].reshape(Q_LORA, -1),
                                     wuq[:, :, MLA_NOPE:].reshape(Q_LORA, -1)], axis=1).astype(BF16)
            wuk_t = jnp.transpose(mla_w_uk[j], (1, 2, 0)).astype(BF16)
            wuv_t = jnp.transpose(mla_w_uv[j], (1, 0, 2)).astype(BF16)
            wuk_f = mla_w_uk[j].reshape(KV_LORA, MLA_HEADS * MLA_NOPE).astype(BF16)
            wuv_f = mla_w_uv[j].reshape(KV_LORA, MLA_HEADS * MLA_V).astype(BF16)
            if len(xs) == 1:
                xs = (xs[0][:seq], xs[0][seq:])
            (qf, kf, vf, qlat, qpe, ckv16s, kpe16s, ckv32, kpe32, dq16, dk32, dk16, dv32, dv16) = _proj0(
                xs[0], xs[1], g[0:1], win16, mla_q_norm[j].reshape(1, -1), wuq16, mla_kv_norm[j].reshape(1, -1),
                wuk_t, wuk_f, wuv_f, inv_tab, past=past, t_s=t_s, tm=tm)
            lam_p = diff_lambda[j]
            sub = diff_subln[j].reshape(1, -1)

            omla_p = _mla_prompt(qf, kf, vf, tq=tq_m, tk=tq_m)
            od_p = _diff_prompt(dq16, dk16, dv16, lam_p, sub, seq=seq, tq=tq_d, tk=tk_d, lam_init=lam_init)

            qlat_s = jnp.transpose(qlat.reshape(MLA_HEADS, b, t_s, KV_LORA), (1, 0, 2, 3)).reshape(b, MLA_HEADS * t_s, KV_LORA)
            qpe_s = jnp.transpose(qpe.reshape(MLA_HEADS, b, t_s, MLA_ROPE), (1, 0, 2, 3)).reshape(b, MLA_HEADS * t_s, MLA_ROPE)
            dq_r = dq16[seq:].reshape(b, t_s, DIFF_KV_HEADS, DIFF_REP, 2, DIFF_HD)
            dq_r = jnp.transpose(dq_r, (0, 2, 4, 3, 1, 5))
            zq = jnp.zeros_like(dq_r[:, :, 0])
            dq_s = jnp.stack([jnp.concatenate([dq_r[:, :, 0], zq], axis=-1),
                              jnp.concatenate([zq, dq_r[:, :, 1]], axis=-1)], axis=2)
            dq_s = dq_s.reshape(b, DIFF_KV_HEADS * 2 * DIFF_REP * t_s, 2 * DIFF_HD)
            npad = 16

            def new_rows(a):
                a = a.reshape(b, t_s, a.shape[-1])
                return jnp.pad(a, ((0, 0), (0, npad - t_s), (0, 0)))

            def interleaved(a):
                return a.reshape(a.shape[:-2] + (a.shape[-2] * DIFF_KV_HEADS, a.shape[-1] // DIFF_KV_HEADS))

            n_pool = cache_diff_k.shape[1]
            olat_s, od_s = _sample_attn(
                page_table, qlat_s, qpe_s, dq_s, new_rows(ckv16s), new_rows(kpe16s),
                interleaved(new_rows(dk16[seq:])), interleaved(new_rows(dv16[seq:])),
                lam_p, sub, cache_mla_ckv, jnp.swapaxes(cache_mla_kpe, 2, 3),
                cache_diff_k.reshape(cache_diff_k.shape[0], n_pool, page * DIFF_KV_HEADS, 2 * DIFF_HD),
                cache_diff_v.reshape(cache_diff_v.shape[0], n_pool, page * DIFF_KV_HEADS, DIFF_VD),
                layer=j, pages=pages, t_s=t_s, lam_init=lam_init)
            olat_t = jnp.transpose(olat_s.reshape(b, MLA_HEADS, t_s, KV_LORA), (1, 0, 2, 3)).reshape(MLA_HEADS, b * t_s, KV_LORA)
            omla_s = _head_proj(olat_t, wuv_t)
            od_s = jnp.transpose(od_s.reshape(b, DIFF_KV_HEADS, DIFF_REP, t_s, DIFF_VD), (0, 3, 1, 2, 4)).reshape(b * t_s, DIFF_HEADS * DIFF_VD)
            mixed = (omla_p, omla_s, od_p, od_s)
            w_out16 = mla_diff_w_out[j].astype(BF16)

            outs.setdefault("ckv_p", []).append(ckv32[:seq].reshape(bp, seq, KV_LORA))
            outs.setdefault("kpe_p", []).append(kpe32[:seq].reshape(bp, seq, MLA_ROPE))
            outs.setdefault("k_p", []).append(dk32[:seq].reshape(bp, seq, DIFF_KV_HEADS, 2 * DIFF_HD))
            outs.setdefault("v_p", []).append(dv32[:seq].reshape(bp, seq, DIFF_KV_HEADS, DIFF_VD))
            outs.setdefault("ckv_s", []).append(ckv32[seq:].reshape(b, t_s, KV_LORA))
            outs.setdefault("kpe_s", []).append(kpe32[seq:].reshape(b, t_s, MLA_ROPE))
            outs.setdefault("k_s", []).append(dk32[seq:].reshape(b, t_s, DIFF_KV_HEADS, 2 * DIFF_HD))
            outs.setdefault("v_s", []).append(dv32[seq:].reshape(b, t_s, DIFF_KV_HEADS, DIFF_VD))
        else:
            if len(xs) == 2:
                xs = (jnp.concatenate(xs, axis=0),)
            hp, u, v32, v16 = _proj1(xs[0], g[0:1], pool_gmlp_w_in[j].astype(BF16), gmlp_norm[j].reshape(1, -1),
                                     tm=tm)
            pw16 = pool_w[j].astype(BF16)
            ps = pool_scale[j].reshape(1, -1)
            tm1 = _pick_tile(seq, (256, 128))
            po_p, go_p = _mix1_prompt(hp, u, v16, pw16, ps, gmlp_ws[j], gmlp_b[j].T, seq=seq, tm=tm1)

            def steps_first(a):
                return jnp.transpose(a[seq:].reshape(b, t_s, a.shape[-1]), (1, 0, 2))

            def rows_first(a):
                return jnp.transpose(a, (1, 0, 2)).reshape(ns, a.shape[-1])

            ws_small = gmlp_ws[j][:, :t_s, :t_s].astype(BF16).astype(F32)
            po_s, go_s = _mix1_sample(ws_small, gmlp_b[j][:, :t_s], jnp.transpose(state_pool[j], (1, 0, 2)),
                                      steps_first(hp), steps_first(u), steps_first(v16), pw16, ps, t_s=t_s, past=past)
            mixed = (po_p, rows_first(po_s), go_p, rows_first(go_s))
            w_out16 = pool_gmlp_w_out[j].astype(BF16)

            hp_s = hp[seq:].reshape(b, t_s, POOL_WIDTH)
            outs.setdefault("pool_p", []).append(hp[seq - POOL_BUF:seq].reshape(bp, POOL_BUF, POOL_WIDTH))
            outs.setdefault("pool_s", []).append(jnp.concatenate([state_pool[j], hp_s], axis=1)[:, -POOL_BUF:])
            outs.setdefault("gv_s", []).append(v32[seq:].reshape(b, t_s, GMLP_WIDTH))

        x = _out_proj(*mixed, w_out16, g[1:2], xs, tm=tm)
        xs = (_mlp(x, g[2:3], g[3:4], wup16, wdn16, layer=i, tm=tm_mlp, tf=tf),)

    x = xs[0]
    st = lambda key: jnp.stack(outs[key])
    return (x[:seq].reshape(bp, seq, d), x[seq:].reshape(b, t_s, d),
            st("ckv_p"), st("kpe_p"), st("k_p"), st("v_p"), st("pool_p"),
            st("ckv_s"), st("kpe_s"), st("k_s"), st("v_s"), st("pool_s"), st("gv_s"))
```

```python
import functools
import math

import jax
import jax.numpy as jnp
from jax import lax
from jax.experimental import pallas as pl
from jax.experimental.pallas import tpu as pltpu

F32 = jnp.float32
BF16 = jnp.bfloat16

EPS = 1e-6
ROPE_THETA = 10000.0
MLA_HEADS = 8
MLA_NOPE = 128
MLA_ROPE = 64
MLA_V = 128
Q_LORA = 512
KV_LORA = 512
MLA_SCALE = (MLA_NOPE + MLA_ROPE) ** -0.5
DIFF_HEADS = 8
DIFF_KV_HEADS = 2
DIFF_REP = DIFF_HEADS // DIFF_KV_HEADS
DIFF_HD = 64
DIFF_VD = 2 * DIFF_HD
DIFF_SCALE = DIFF_HD ** -0.5
POOL_WINDOWS = (2, 4, 8, 16)
POOL_GROUP = 256
POOL_WIDTH = len(POOL_WINDOWS) * POOL_GROUP
POOL_BUF = max(POOL_WINDOWS) - 1
CHUNK = 128
GMLP_GROUPS = 4
GMLP_GROUP = 256
GMLP_WIDTH = GMLP_GROUPS * GMLP_GROUP

LOG2E = 1.4426950408889634
NEG_BIG = -1e30
LANES = 128
V7X_VMEM_BUDGET = 56 * 1024 * 1024

_Z_CQ = 0
_Z_CKV = _Z_CQ + Q_LORA
_Z_DQ = _Z_CKV + KV_LORA
_Z_DK = _Z_DQ + DIFF_HEADS * 2 * DIFF_HD
_Z_DV = _Z_DK + DIFF_KV_HEADS * 2 * DIFF_HD
_Z_KPE = _Z_DV + DIFF_KV_HEADS * DIFF_VD
_Z_END = _Z_KPE + LANES


def _cparams(sem, vmem_bytes):
    return pltpu.CompilerParams(dimension_semantics=sem,
                                vmem_limit_bytes=int(min(vmem_bytes, V7X_VMEM_BUDGET)))


def _rms(x, g):
    return x * lax.rsqrt(jnp.mean(x * x, axis=-1, keepdims=True) + EPS) * g


def _dot(a, b):
    return jnp.dot(a, b, preferred_element_type=F32)


def _dot_nt(a, b):
    return lax.dot_general(a, b, (((1,), (1,)), ((), ())), preferred_element_type=F32)


def _rope128(x, cos, sin_signed, first_half):
    swapped = jnp.where(first_half, pltpu.roll(x, LANES - 32, axis=1), pltpu.roll(x, 32, axis=1))
    return x * cos + swapped * sin_signed


def _row_specs(tm, width, n_p):
    return [pl.BlockSpec((tm, width), lambda i, *_: (jnp.minimum(i, n_p - 1), 0)),
            pl.BlockSpec((tm, width), lambda i, *_: (jnp.maximum(i - n_p, 0), 0))]


def _read_rows(i, n_p, p_ref, s_ref):
    return jnp.where(i < n_p, p_ref[...], s_ref[...])


_QK_W = 2 * LANES


def _proj0_kernel(xp_ref, xs_ref, g_ref, win_ref, qn_ref, wuq_ref, kvn_ref, wuk_ref, wukf_ref, wuvf_ref, inv_ref,
                  qf_ref, kf_ref, vf_ref, qlat_ref, qpe_ref, ckv16_ref, kpe16_ref,
                  ckv32_ref, kpe32_ref, dq_ref, dk32_ref, dk16_ref, dv32_ref, dv16_ref,
                  *, tm, n_p, seq, past, t_s):
    i = pl.program_id(0)
    h = _rms(_read_rows(i, n_p, xp_ref, xs_ref), g_ref[...]).astype(BF16)
    z = _dot(h, win_ref[...])

    row = lax.broadcasted_iota(jnp.int32, (tm, LANES), 0) + i * tm
    pos = jnp.where(row < seq, row, past + lax.rem(row - seq, t_s)).astype(F32)
    ang = pos * inv_ref[...]
    cos = jnp.cos(ang)
    sin = jnp.sin(ang)
    lane = lax.broadcasted_iota(jnp.int32, (tm, LANES), 1)
    first_half = (lane & 32) == 0
    sin_signed = jnp.where(first_half, -sin, sin)

    ckv = _rms(z[:, _Z_CKV:_Z_CKV + KV_LORA], kvn_ref[...])
    ckv32_ref[...] = ckv
    ckv16 = ckv.astype(BF16)
    kpe = _rope128(z[:, _Z_KPE:_Z_KPE + LANES], cos, sin_signed, first_half)[:, :MLA_ROPE]
    kpe32_ref[...] = kpe
    kpe16 = kpe.astype(BF16)

    dq_ref[...] = (z[:, _Z_DQ:_Z_DK] * DIFF_SCALE).astype(BF16)
    dk = z[:, _Z_DK:_Z_DV]
    dk32_ref[...] = dk
    dk16_ref[...] = dk.astype(BF16)
    dv = z[:, _Z_DV:_Z_KPE]
    dv32_ref[...] = dv
    dv16_ref[...] = dv.astype(BF16)

    cq = _rms(z[:, _Z_CQ:_Z_CQ + Q_LORA], qn_ref[...]).astype(BF16)
    q = _dot(cq, wuq_ref[...])
    pe0 = MLA_HEADS * MLA_NOPE
    q_pe = [_rope128(q[:, pe0 + c * LANES:pe0 + (c + 1) * LANES], cos, sin_signed, first_half).astype(BF16)
            for c in range(MLA_HEADS * MLA_ROPE // LANES)]

    @pl.when(i < n_p)
    def _():
        kn = _dot(ckv16, wukf_ref[...])
        vf_ref[...] = _dot(ckv16, wuvf_ref[...]).astype(BF16)
        zero = jnp.zeros((tm, _QK_W - MLA_NOPE - MLA_ROPE), BF16)
        for hd in range(MLA_HEADS):
            cols = slice(hd * MLA_NOPE, (hd + 1) * MLA_NOPE)
            half = slice((hd % 2) * MLA_ROPE, (hd % 2 + 1) * MLA_ROPE)
            kf_ref[hd, :, :MLA_NOPE] = kn[:, cols].astype(BF16)
            kf_ref[hd, :, MLA_NOPE:MLA_NOPE + MLA_ROPE] = kpe16
            kf_ref[hd, :, MLA_NOPE + MLA_ROPE:] = zero
            qf_ref[hd, :, :MLA_NOPE] = q[:, cols].astype(BF16)
            qf_ref[hd, :, MLA_NOPE:MLA_NOPE + MLA_ROPE] = q_pe[hd // 2][:, half]
            qf_ref[hd, :, MLA_NOPE + MLA_ROPE:] = zero

    @pl.when(i >= n_p)
    def _():
        ckv16_ref[...] = ckv16
        kpe16_ref[...] = kpe16
        for hd in range(MLA_HEADS):
            qn = q[:, hd * MLA_NOPE:(hd + 1) * MLA_NOPE].astype(BF16)
            qlat_ref[hd] = _dot(qn, wuk_ref[hd]).astype(BF16)
            qpe_ref[hd] = q_pe[hd // 2][:, (hd % 2) * MLA_ROPE:(hd % 2 + 1) * MLA_ROPE]


def _proj0(xp, xs, g, win, qn, wuq, kvn, wuk_t, wuk_f, wuv_f, inv, *, past, t_s, tm):
    seq, d = xp.shape
    ns = xs.shape[0]
    n = seq + ns
    n_p = seq // tm
    const2 = lambda i: (0, 0)
    row2 = lambda i: (i, 0)
    prow3 = lambda i: (0, jnp.minimum(i, n_p - 1), 0)
    srow3 = lambda i: (0, jnp.maximum(i - n_p, 0), 0)
    prow2 = lambda i: (jnp.minimum(i, n_p - 1), 0)
    srow2 = lambda i: (jnp.maximum(i - n_p, 0), 0)
    kw = DIFF_KV_HEADS * 2 * DIFF_HD
    vw = DIFF_KV_HEADS * DIFF_VD
    out_shape = (
        jax.ShapeDtypeStruct((MLA_HEADS, seq, _QK_W), BF16),
        jax.ShapeDtypeStruct((MLA_HEADS, seq, _QK_W), BF16),
        jax.ShapeDtypeStruct((seq, MLA_HEADS * MLA_V), BF16),
        jax.ShapeDtypeStruct((MLA_HEADS, ns, KV_LORA), BF16),
        jax.ShapeDtypeStruct((MLA_HEADS, ns, MLA_ROPE), BF16),
        jax.ShapeDtypeStruct((ns, KV_LORA), BF16),
        jax.ShapeDtypeStruct((ns, MLA_ROPE), BF16),
        jax.ShapeDtypeStruct((n, KV_LORA), F32),
        jax.ShapeDtypeStruct((n, MLA_ROPE), F32),
        jax.ShapeDtypeStruct((n, DIFF_HEADS * 2 * DIFF_HD), BF16),
        jax.ShapeDtypeStruct((n, kw), F32),
        jax.ShapeDtypeStruct((n, kw), BF16),
        jax.ShapeDtypeStruct((n, vw), F32),
        jax.ShapeDtypeStruct((n, vw), BF16),
    )
    out_specs = (
        pl.BlockSpec((MLA_HEADS, tm, _QK_W), prow3),
        pl.BlockSpec((MLA_HEADS, tm, _QK_W), prow3),
        pl.BlockSpec((tm, MLA_HEADS * MLA_V), prow2),
        pl.BlockSpec((MLA_HEADS, tm, KV_LORA), srow3),
        pl.BlockSpec((MLA_HEADS, tm, MLA_ROPE), srow3),
        pl.BlockSpec((tm, KV_LORA), srow2),
        pl.BlockSpec((tm, MLA_ROPE), srow2),
        pl.BlockSpec((tm, KV_LORA), row2),
        pl.BlockSpec((tm, MLA_ROPE), row2),
        pl.BlockSpec((tm, DIFF_HEADS * 2 * DIFF_HD), row2),
        pl.BlockSpec((tm, kw), row2),
        pl.BlockSpec((tm, kw), row2),
        pl.BlockSpec((tm, vw), row2),
        pl.BlockSpec((tm, vw), row2),
    )
    in_specs = _row_specs(tm, d, n_p) + [
        pl.BlockSpec((1, d), const2),
        pl.BlockSpec(win.shape, const2),
        pl.BlockSpec((1, Q_LORA), const2),
        pl.BlockSpec(wuq.shape, const2),
        pl.BlockSpec((1, KV_LORA), const2),
        pl.BlockSpec(wuk_t.shape, lambda i: (0, 0, 0)),
        pl.BlockSpec(wuk_f.shape, const2),
        pl.BlockSpec(wuv_f.shape, const2),
        pl.BlockSpec((1, LANES), const2),
    ]
    weights = win.size + wuq.size + wuk_t.size + wuk_f.size + wuv_f.size
    vmem = 2 * weights * 2 + 4 * tm * d * 4 + 16 * tm * _Z_END * 4
    return pl.pallas_call(
        functools.partial(_proj0_kernel, tm=tm, n_p=n_p, seq=seq, past=past, t_s=t_s),
        grid=(n // tm,), in_specs=in_specs, out_specs=out_specs, out_shape=out_shape,
        compiler_params=_cparams(("arbitrary",), vmem), name="proj0",
    )(xp, xs, g, win, qn, wuq, kvn, wuk_t, wuk_f, wuv_f, inv)


def _lanes_to(x, width):
    if width <= LANES:
        return x[:, :width]
    return jnp.concatenate([x] * (width // LANES), axis=1)


def _softmax_update(s2, v16, m_ref, l_ref, acc_ref):
    m_prev = m_ref[...]
    m_new = jnp.maximum(m_prev, jnp.max(s2, axis=-1, keepdims=True))
    p = jnp.exp2(s2 - _lanes_to(m_new, s2.shape[1]))
    alpha = jnp.exp2(m_prev - m_new)
    l_ref[...] = alpha * l_ref[...] + jnp.sum(p, axis=-1, keepdims=True)
    acc_ref[...] = _lanes_to(alpha, acc_ref.shape[-1]) * acc_ref[...] + _dot(p.astype(BF16), v16)
    m_ref[...] = m_new


def _softmax_finish(acc_ref, l_ref):
    return acc_ref[...] / _lanes_to(l_ref[...], acc_ref.shape[-1])


def _diff_lambda(lam_ref, lam_init):
    lp = lam_ref[...]
    a = jnp.sum(lp[0:1] * lp[1:2], axis=-1, keepdims=True)
    b = jnp.sum(lp[2:3] * lp[3:4], axis=-1, keepdims=True)
    return jnp.exp(a) - jnp.exp(b) + lam_init


_HEADS_PER_STEP = 4


def _mla_prompt_kernel(q_ref, k_ref, v_ref, o_ref, m_ref, l_ref, acc_ref, *, tq, tk):
    i = pl.program_id(1)
    m_ref[...] = jnp.full(m_ref.shape, NEG_BIG, F32)
    l_ref[...] = jnp.zeros(l_ref.shape, F32)
    acc_ref[...] = jnp.zeros(acc_ref.shape, F32)
    c = MLA_SCALE * LOG2E

    def block(j, masked):
        k0 = pl.multiple_of(j * tk, tk)
        s2s = [_dot_nt(q_ref[hh], k_ref[hh, pl.ds(k0, tk), :]) * c for hh in range(_HEADS_PER_STEP)]
        if masked:
            qpos = i * tq + lax.broadcasted_iota(jnp.int32, (tq, tk), 0)
            kpos = j * tk + lax.broadcasted_iota(jnp.int32, (tq, tk), 1)
            s2s = [jnp.where(kpos <= qpos, s2, NEG_BIG) for s2 in s2s]
        for hh, s2 in enumerate(s2s):
            _softmax_update(s2, v_ref[pl.ds(k0, tk), hh * MLA_V:(hh + 1) * MLA_V],
                            m_ref.at[hh], l_ref.at[hh], acc_ref.at[hh])

    def full_block(j, carry):
        block(j, False)
        return carry

    jd = (i * tq) // tk
    lax.fori_loop(0, jd, full_block, 0)
    block(jd, True)
    for hh in range(_HEADS_PER_STEP):
        o_ref[:, hh * MLA_V:(hh + 1) * MLA_V] = _softmax_finish(acc_ref.at[hh], l_ref.at[hh]).astype(BF16)


def _mla_prompt(qf, kf, vf, *, tq, tk):
    heads, seq, w = qf.shape
    hps = _HEADS_PER_STEP
    assert heads % hps == 0 and tk % tq == 0
    vmem = (hps * seq * (w + MLA_V) * 2 + 4 * hps * tq * (w + MLA_V) * 2
            + hps * tq * (2 * LANES + MLA_V) * 4 + 6 * hps * tq * tk * 4)
    return pl.pallas_call(
        functools.partial(_mla_prompt_kernel, tq=tq, tk=tk),
        grid=(heads // hps, seq // tq),
        in_specs=[
            pl.BlockSpec((hps, tq, w), lambda hp, i: (hp, i, 0)),
            pl.BlockSpec((hps, seq, w), lambda hp, i: (hp, 0, 0), pipeline_mode=pl.Buffered(1)),
            pl.BlockSpec((seq, hps * MLA_V), lambda hp, i: (0, hp), pipeline_mode=pl.Buffered(1)),
        ],
        out_specs=pl.BlockSpec((tq, hps * MLA_V), lambda hp, i: (i, hp)),
        out_shape=jax.ShapeDtypeStruct((seq, heads * MLA_V), BF16),
        scratch_shapes=[pltpu.VMEM((hps, tq, LANES), F32), pltpu.VMEM((hps, tq, LANES), F32),
                        pltpu.VMEM((hps, tq, MLA_V), F32)],
        compiler_params=_cparams(("parallel", "parallel"), vmem), name="mla_prompt",
    )(qf, kf, vf)


def _diff_finish(a, lam, sub, scale_out):
    half = a.shape[0] // 2
    o = a[:half] - lam * a[half:]
    return _rms(o, sub) * scale_out


_ITEM_GROUP = 1
_ITEM_DIAG = 2
_ITEM_FIRST = 4


def _diff_prompt_schedule(seq, tq, tk, total_steps):
    nq = seq // tq
    counts = [DIFF_KV_HEADS * ((q * tq) // tk + 1) for q in range(nq)]
    max_items = -(-sum(counts) // total_steps)
    steps_q = [-(-c // max_items) for c in counts]
    spare = total_steps - sum(steps_q)
    assert spare >= 0
    while spare > 0:
        q = max(range(nq), key=lambda t: counts[t] / steps_q[t])
        if steps_q[q] >= counts[q]:
            break
        steps_q[q] += 1
        spare -= 1
    step_start, step_q, item_j, item_fl = [0], [], [], []
    for q in range(nq):
        jd = (q * tq) // tk
        base = len(item_j)
        for j in range(jd + 1):
            for g in range(DIFF_KV_HEADS):
                item_j.append(j)
                item_fl.append(g * _ITEM_GROUP + (_ITEM_DIAG if j == jd else 0)
                               + (_ITEM_FIRST if j == 0 and g == 0 else 0))
        for k in range(steps_q[q]):
            step_q.append(q)
            step_start.append(base + ((k + 1) * counts[q]) // steps_q[q])
    while len(step_q) < total_steps:
        step_q.append(nq - 1)
        step_start.append(len(item_j))
    as_i32 = lambda a: jnp.asarray(a, dtype=jnp.int32)
    return as_i32(step_start), as_i32(step_q), as_i32(item_j), as_i32(item_fl), max_items


def _diff_prompt_start(dq_ref, q_scr, m_ref, l_ref, acc_ref, tq):
    lane = lax.broadcasted_iota(jnp.int32, (tq, 2 * DIFF_HD), 1)
    for g in range(DIFF_KV_HEADS):
        for r in range(DIFF_REP):
            hd = g * DIFF_REP + r
            qh = dq_ref[:, hd * 2 * DIFF_HD:(hd + 1) * 2 * DIFF_HD]
            zero = jnp.zeros_like(qh)
            q_scr[g, r * tq:(r + 1) * tq, :] = jnp.where(lane < DIFF_HD, qh, zero)
            q_scr[g, (DIFF_REP + r) * tq:(DIFF_REP + r + 1) * tq, :] = jnp.where(lane < DIFF_HD, zero, qh)
    m_ref[...] = jnp.full(m_ref.shape, NEG_BIG, F32)
    l_ref[...] = jnp.zeros(l_ref.shape, F32)
    acc_ref[...] = jnp.zeros(acc_ref.shape, F32)


def _diff_prompt_item(q, j, g, diag, q_scr, dk_ref, dv_ref, m_ref, l_ref, acc_ref, lam_ref, sub_ref, o_ref,
                      *, tq, tk, lam_init):
    rows = 2 * DIFF_REP * tq
    k0 = pl.multiple_of(j * tk, tk)
    s2 = _dot_nt(q_scr[g], dk_ref[pl.ds(k0, tk), g * 2 * DIFF_HD:(g + 1) * 2 * DIFF_HD]) * LOG2E
    if diag:
        qpos = q * tq + (lax.broadcasted_iota(jnp.int32, (rows, tk), 0) & (tq - 1))
        kpos = j * tk + lax.broadcasted_iota(jnp.int32, (rows, tk), 1)
        s2 = jnp.where(kpos <= qpos, s2, NEG_BIG)
    _softmax_update(s2, dv_ref[pl.ds(k0, tk), g * DIFF_VD:(g + 1) * DIFF_VD],
                    m_ref.at[g], l_ref.at[g], acc_ref.at[g])
    if diag:
        lam = _diff_lambda(lam_ref, lam_init)
        o = _diff_finish(_softmax_finish(acc_ref.at[g], l_ref.at[g]), lam, sub_ref[...], 1.0 - lam_init)
        for r in range(DIFF_REP):
            hd = g * DIFF_REP + r
            o_ref[:, hd * DIFF_VD:(hd + 1) * DIFF_VD] = o[r * tq:(r + 1) * tq].astype(BF16)


_N_CACHES = 4


def _sample_attn_kernel(pt_ref, sstart_ref, sq_ref, ij_ref, ifl_ref,
                        qlat_ref, qpe_ref, dq_ref, ckvn_ref, kpen_ref, dkn_ref, dvn_ref,
                        lam_ref, sub_ref, pdq_ref, pdk_ref, pdv_ref, ckv_hbm, kpe_hbm, dk_hbm, dv_hbm,
                        olat_ref, od_ref, pod_ref,
                        ckv_buf, kpe_buf, dk_buf, dv_buf, sems,
                        m1_ref, l1_ref, acc1_ref, m2_ref, l2_ref, acc2_ref,
                        pq_scr, pm_ref, pl_ref, pacc_ref,
                        *, pages, layer, t_s, lam_init, tq, tk, max_items):
    steps = pl.num_programs(1)
    s = pl.program_id(1)
    n = pl.program_id(0) * steps + s
    total = pl.num_programs(0) * steps
    slot = lax.rem(n, 2)

    def page_copies(step, slot_):
        cps = []
        for kk in range(pages):
            pid = pt_ref[step * pages + kk]
            for ci, (hbm, buf) in enumerate(((ckv_hbm, ckv_buf), (kpe_hbm, kpe_buf),
                                             (dk_hbm, dk_buf), (dv_hbm, dv_buf))):
                cps.append(pltpu.make_async_copy(hbm.at[layer, pid], buf.at[slot_, kk], sems.at[slot_, ci]))
        return cps

    @pl.when(n == 0)
    def _():
        for cp in page_copies(0, 0):
            cp.start()

    @pl.when(n + 1 < total)
    def _():
        for cp in page_copies(n + 1, 1 - slot):
            cp.start()

    ql = qlat_ref[...]
    qp = qpe_ref[...]
    dq = dq_ref[...]
    c = MLA_SCALE * LOG2E
    rows1 = ql.shape[0]
    rows2 = dq.shape[0]
    grows = rows2 // DIFF_KV_HEADS

    def group_mask(keys):
        grp = lax.broadcasted_iota(jnp.int32, (rows2, keys), 0) // grows
        return (lax.broadcasted_iota(jnp.int32, (rows2, keys), 1) & (DIFF_KV_HEADS - 1)) == grp

    @pl.when(s == 0)
    def _():
        m1_ref[...] = jnp.full(m1_ref.shape, NEG_BIG, F32)
        l1_ref[...] = jnp.zeros(l1_ref.shape, F32)
        acc1_ref[...] = jnp.zeros(acc1_ref.shape, F32)
        m2_ref[...] = jnp.full(m2_ref.shape, NEG_BIG, F32)
        l2_ref[...] = jnp.zeros(l2_ref.shape, F32)
        acc2_ref[...] = jnp.zeros(acc2_ref.shape, F32)
        kn = ckvn_ref[...]
        npad = kn.shape[0]
        s2 = (_dot_nt(ql, kn) + _dot_nt(qp, kpen_ref[...])) * c
        step1 = lax.rem(lax.broadcasted_iota(jnp.int32, (rows1, npad), 0), t_s)
        key1 = lax.broadcasted_iota(jnp.int32, (rows1, npad), 1)
        _softmax_update(jnp.where(key1 <= step1, s2, NEG_BIG), kn, m1_ref, l1_ref, acc1_ref)
        dkn = dkn_ref[...]
        nk = dkn.shape[0]
        sd = _dot_nt(dq, dkn) * LOG2E
        step2 = lax.rem(lax.broadcasted_iota(jnp.int32, (rows2, nk), 0), t_s)
        key2 = lax.broadcasted_iota(jnp.int32, (rows2, nk), 1) // DIFF_KV_HEADS
        ok = jnp.logical_and(group_mask(nk), key2 <= step2)
        _softmax_update(jnp.where(ok, sd, NEG_BIG), dvn_ref[...], m2_ref, l2_ref, acc2_ref)

    for cp in page_copies(n, slot):
        cp.wait()

    k = jnp.concatenate([ckv_buf[slot, kk].astype(BF16) for kk in range(pages)], axis=0)
    kp_t = jnp.concatenate([kpe_buf[slot, kk].astype(BF16) for kk in range(pages)], axis=1)
    dk = jnp.concatenate([dk_buf[slot, kk].astype(BF16) for kk in range(pages)], axis=0)
    dv = jnp.concatenate([dv_buf[slot, kk].astype(BF16) for kk in range(pages)], axis=0)
    s2 = (_dot_nt(ql, k) + _dot(qp, kp_t)) * c
    sd = jnp.where(group_mask(dk.shape[0]), _dot_nt(dq, dk) * LOG2E, NEG_BIG)
    _softmax_update(s2, k, m1_ref, l1_ref, acc1_ref)
    _softmax_update(sd, dv, m2_ref, l2_ref, acc2_ref)

    @pl.when(s == steps - 1)
    def _():
        olat_ref[...] = _softmax_finish(acc1_ref, l1_ref).astype(BF16)
        lam = _diff_lambda(lam_ref, lam_init)
        a = _softmax_finish(acc2_ref, l2_ref)
        for g in range(DIFF_KV_HEADS):
            od_ref[g] = _diff_finish(a[g * grows:(g + 1) * grows], lam, sub_ref[...], 1.0 - lam_init).astype(BF16)

    q_tile = sq_ref[n]
    item0 = sstart_ref[n]
    item_end = sstart_ref[n + 1]
    for r in range(max_items):
        item = item0 + r

        @pl.when(item < item_end)
        def _():
            j = ij_ref[item]
            fl = ifl_ref[item]

            @pl.when((fl & _ITEM_FIRST) != 0)
            def _():
                _diff_prompt_start(pdq_ref, pq_scr, pm_ref, pl_ref, pacc_ref, tq)

            for g in range(DIFF_KV_HEADS):
                for diag in (False, True):
                    @pl.when((fl & (_ITEM_GROUP | _ITEM_DIAG)) == g * _ITEM_GROUP + (_ITEM_DIAG if diag else 0))
                    def _():
                        _diff_prompt_item(q_tile, j, g, diag, pq_scr, pdk_ref, pdv_ref, pm_ref, pl_ref, pacc_ref,
                                          lam_ref, sub_ref, pod_ref, tq=tq, tk=tk, lam_init=lam_init)


def _sample_attn(page_table, qlat_s, qpe_s, dq_s, ckvn, kpen, dkn, dvn, lam_p, sub, pdq, pdk, pdv,
                 cache_ckv, cache_kpe, cache_k, cache_v, *, seq, layer, pages, t_s, lam_init, tq, tk):
    b, n_pages = page_table.shape
    assert n_pages % pages == 0 and tk % tq == 0
    steps = n_pages // pages
    rows1 = qlat_s.shape[1]
    rows2 = dq_s.shape[1]
    grows = rows2 // DIFF_KV_HEADS
    prows = 2 * DIFF_REP * tq
    sstart, sq, ij, ifl, max_items = _diff_prompt_schedule(seq, tq, tk, b * steps)

    def seq3(a):
        return pl.BlockSpec((None,) + a.shape[1:], lambda bi, si, *_: (bi, 0, 0))

    const2 = lambda bi, si, *_: (0, 0)
    tile2 = lambda bi, si, pt, ss, sq_, *_: (sq_[bi * steps + si], 0)
    hbm = pl.BlockSpec(memory_space=pl.ANY)
    in_specs = [
        seq3(qlat_s), seq3(qpe_s), seq3(dq_s), seq3(ckvn), seq3(kpen), seq3(dkn), seq3(dvn),
        pl.BlockSpec(lam_p.shape, const2),
        pl.BlockSpec(sub.shape, const2),
        pl.BlockSpec((tq, pdq.shape[1]), tile2),
        pl.BlockSpec((seq, pdk.shape[1]), const2, pipeline_mode=pl.Buffered(1)),
        pl.BlockSpec((seq, pdv.shape[1]), const2, pipeline_mode=pl.Buffered(1)),
        hbm, hbm, hbm, hbm,
    ]
    out_specs = (
        pl.BlockSpec((None, rows1, KV_LORA), lambda bi, si, *_: (bi, 0, 0)),
        pl.BlockSpec((None, DIFF_KV_HEADS, grows // 2, DIFF_VD), lambda bi, si, *_: (bi, 0, 0, 0)),
        pl.BlockSpec((tq, DIFF_HEADS * DIFF_VD), tile2),
    )
    out_shape = (
        jax.ShapeDtypeStruct((b, rows1, KV_LORA), BF16),
        jax.ShapeDtypeStruct((b, DIFF_KV_HEADS, grows // 2, DIFF_VD), BF16),
        jax.ShapeDtypeStruct((seq, DIFF_HEADS * DIFF_VD), BF16),
    )
    bufs = [pltpu.VMEM((2, pages) + c.shape[2:], c.dtype) for c in (cache_ckv, cache_kpe, cache_k, cache_v)]
    scratch = bufs + [
        pltpu.SemaphoreType.DMA((2, _N_CACHES)),
        pltpu.VMEM((rows1, LANES), F32), pltpu.VMEM((rows1, LANES), F32), pltpu.VMEM((rows1, KV_LORA), F32),
        pltpu.VMEM((rows2, LANES), F32), pltpu.VMEM((rows2, LANES), F32), pltpu.VMEM((rows2, DIFF_VD), F32),
        pltpu.VMEM((DIFF_KV_HEADS, prows, 2 * DIFF_HD), BF16),
        pltpu.VMEM((DIFF_KV_HEADS, prows, LANES), F32),
        pltpu.VMEM((DIFF_KV_HEADS, prows, LANES), F32),
        pltpu.VMEM((DIFF_KV_HEADS, prows, DIFF_VD), F32)]
    step_bytes = pages * sum(math.prod(c.shape[2:]) for c in (cache_ckv, cache_kpe, cache_k, cache_v)) * 4
    prompt_bytes = (seq * (pdk.shape[1] + pdv.shape[1]) * 2 + 8 * tq * pdq.shape[1] * 2
                    + DIFF_KV_HEADS * prows * (2 * DIFF_HD * 2 + (2 * LANES + DIFF_VD) * 4) + 5 * prows * tk * 4)
    vmem = 2 * step_bytes + 2 * step_bytes + 6 * 1024 * 1024 + prompt_bytes
    grid_spec = pltpu.PrefetchScalarGridSpec(
        num_scalar_prefetch=5, grid=(b, steps), in_specs=in_specs, out_specs=out_specs,
        scratch_shapes=scratch)
    return pl.pallas_call(
        functools.partial(_sample_attn_kernel, pages=pages, layer=layer, t_s=t_s, lam_init=lam_init,
                          tq=tq, tk=tk, max_items=max_items),
        grid_spec=grid_spec, out_shape=out_shape,
        compiler_params=_cparams(("arbitrary", "arbitrary"), vmem), name="sample_attn",
    )(page_table.reshape(-1), sstart, sq, ij, ifl, qlat_s, qpe_s, dq_s, ckvn, kpen, dkn, dvn, lam_p, sub,
      pdq, pdk, pdv, cache_ckv, cache_kpe, cache_k, cache_v)


def _head_proj_kernel(o_ref, w_ref, out_ref):
    out_ref[...] = _dot(o_ref[...], w_ref[...]).astype(BF16)


def _head_proj(o_t, wuv_t):
    heads, rows, r = o_t.shape
    return pl.pallas_call(
        _head_proj_kernel, grid=(heads,),
        in_specs=[pl.BlockSpec((None, rows, r), lambda hd: (hd, 0, 0)),
                  pl.BlockSpec((None, r, MLA_V), lambda hd: (hd, 0, 0))],
        out_specs=pl.BlockSpec((rows, MLA_V), lambda hd: (0, hd)),
        out_shape=jax.ShapeDtypeStruct((rows, heads * MLA_V), BF16),
        compiler_params=_cparams(("parallel",), 16 * 1024 * 1024), name="head_proj",
    )(o_t, wuv_t)


def _out_proj_kernel(ap_ref, as_ref, bp_ref, bs_ref, w_ref, g_ref, *rest, n_p):
    *x_refs, o_ref = rest
    i = pl.program_id(0)
    ka = ap_ref.shape[1]
    mix = (_dot(_read_rows(i, n_p, ap_ref, as_ref), w_ref[:ka, :])
           + _dot(_read_rows(i, n_p, bp_ref, bs_ref), w_ref[ka:, :]))
    x = x_refs[0][...] if len(x_refs) == 1 else _read_rows(i, n_p, *x_refs)
    o_ref[...] = x + _rms(mix, g_ref[...])


def _out_proj(a_p, a_s, b_p, b_s, w16, g, xs, *, tm):
    seq, ka = a_p.shape
    kb = b_p.shape[1]
    n = seq + a_s.shape[0]
    d = w16.shape[1]
    n_p = seq // tm
    x_specs = [pl.BlockSpec((tm, d), lambda i: (i, 0))] if len(xs) == 1 else _row_specs(tm, d, n_p)
    vmem = 2 * (ka + kb) * d * 2 + 4 * tm * (ka + kb) * 2 + 12 * tm * d * 4
    return pl.pallas_call(
        functools.partial(_out_proj_kernel, n_p=n_p), grid=(n // tm,),
        in_specs=(_row_specs(tm, ka, n_p) + _row_specs(tm, kb, n_p)
                  + [pl.BlockSpec((ka + kb, d), lambda i: (0, 0)), pl.BlockSpec((1, d), lambda i: (0, 0))]
                  + x_specs),
        out_specs=pl.BlockSpec((tm, d), lambda i: (i, 0)),
        out_shape=jax.ShapeDtypeStruct((n, d), F32),
        compiler_params=_cparams(("parallel",), vmem), name="out_proj",
    )(a_p, a_s, b_p, b_s, w16, g, *xs)


def _mlp_kernel(x_ref, gpre_ref, gpost_ref, wup_ref, wdn_ref, o_ref, h_scr, acc_scr):
    k = pl.program_id(1)

    @pl.when(k == 0)
    def _():
        h_scr[...] = _rms(x_ref[...], gpre_ref[...]).astype(BF16)
        acc_scr[...] = jnp.zeros(acc_scr.shape, F32)

    a = jnp.maximum(_dot(h_scr[...], wup_ref[...]), 0.0)
    acc_scr[...] += _dot((a * a).astype(BF16), wdn_ref[...])

    @pl.when(k == pl.num_programs(1) - 1)
    def _():
        o_ref[...] = x_ref[...] + _rms(acc_scr[...], gpost_ref[...])


def _mlp(x, gpre, gpost, wup16, wdn16, *, layer, tm, tf):
    n, d = x.shape
    f = wup16.shape[2]
    vmem = 2 * 2 * d * tf * 2 + 4 * tm * d * 4 + tm * d * 2 + tm * d * 4 + 3 * tm * tf * 4 + 2 * tm * d * 4
    return pl.pallas_call(
        _mlp_kernel, grid=(n // tm, f // tf),
        in_specs=[pl.BlockSpec((tm, d), lambda i, k: (i, 0)),
                  pl.BlockSpec((1, d), lambda i, k: (0, 0)),
                  pl.BlockSpec((1, d), lambda i, k: (0, 0)),
                  pl.BlockSpec((None, d, tf), lambda i, k: (layer, 0, k)),
                  pl.BlockSpec((None, tf, d), lambda i, k: (layer, k, 0))],
        out_specs=pl.BlockSpec((tm, d), lambda i, k: (i, 0)),
        out_shape=jax.ShapeDtypeStruct((n, d), F32),
        scratch_shapes=[pltpu.VMEM((tm, d), BF16), pltpu.VMEM((tm, d), F32)],
        compiler_params=_cparams(("parallel", "arbitrary"), vmem), name="mlp",
    )(x, gpre, gpost, wup16, wdn16)


def _proj1_kernel(x_ref, g_ref, win_ref, gn_ref, hp_ref, u_ref, v32_ref, v16_ref):
    h = _rms(x_ref[...], g_ref[...]).astype(BF16)
    z = _dot(h, win_ref[...])
    hp_ref[...] = z[:, :POOL_WIDTH]
    uv = jax.nn.gelu(z[:, POOL_WIDTH:])
    u_ref[...] = uv[:, :GMLP_WIDTH]
    vv = uv[:, GMLP_WIDTH:]
    xc = vv - jnp.mean(vv, axis=-1, keepdims=True)
    v = xc * lax.rsqrt(jnp.mean(xc * xc, axis=-1, keepdims=True) + EPS) * gn_ref[...]
    v32_ref[...] = v
    v16_ref[...] = v.astype(BF16)


def _proj1(x, g, win16, gn, *, tm):
    n, d = x.shape
    w = win16.shape[1]
    row2 = lambda i: (i, 0)
    const2 = lambda i: (0, 0)
    vmem = 2 * win16.size * 2 + 2 * tm * d * 4 + 10 * tm * w * 4
    return pl.pallas_call(
        _proj1_kernel, grid=(n // tm,),
        in_specs=[pl.BlockSpec((tm, d), row2), pl.BlockSpec((1, d), const2),
                  pl.BlockSpec(win16.shape, const2), pl.BlockSpec((1, GMLP_WIDTH), const2)],
        out_specs=(pl.BlockSpec((tm, POOL_WIDTH), row2), pl.BlockSpec((tm, GMLP_WIDTH), row2),
                   pl.BlockSpec((tm, GMLP_WIDTH), row2), pl.BlockSpec((tm, GMLP_WIDTH), row2)),
        out_shape=(jax.ShapeDtypeStruct((n, POOL_WIDTH), F32), jax.ShapeDtypeStruct((n, GMLP_WIDTH), F32),
                   jax.ShapeDtypeStruct((n, GMLP_WIDTH), F32), jax.ShapeDtypeStruct((n, GMLP_WIDTH), BF16)),
        compiler_params=_cparams(("parallel",), vmem), name="proj1",
    )(x, g, win16, gn)


_HALO = 16


def _mix1_prompt_kernel(hp_ref, halo_ref, u_ref, v_ref, pw_ref, ps_ref, ws_ref, bt_ref, po_ref, go_ref, *, tm):
    i = pl.program_id(0)
    halo = jnp.where(i == 0, 0.0, halo_ref[...])
    ext = jnp.concatenate([halo, hp_ref[...]], axis=0)
    pos = i * tm + lax.broadcasted_iota(jnp.int32, (tm, POOL_GROUP), 0)
    for gi, w in enumerate(POOL_WINDOWS):
        run = ext[:, gi * POOL_GROUP:(gi + 1) * POOL_GROUP]
        width = 1
        while width < w:
            run = run + pltpu.roll(run, width, axis=0)
            width *= 2
        wsum = run[_HALO:]
        cnt = jnp.minimum(pos + 1, w).astype(F32)
        d = (wsum / cnt - hp_ref[:, gi * POOL_GROUP:(gi + 1) * POOL_GROUP]).astype(BF16)
        po = _dot(d, pw_ref[gi]) * ps_ref[:, gi * POOL_GROUP:(gi + 1) * POOL_GROUP]
        po_ref[:, gi * POOL_GROUP:(gi + 1) * POOL_GROUP] = po.astype(BF16)

    tril =(lax.broadcasted_iota(jnp.int32, (CHUNK, CHUNK), 1)
            <= lax.broadcasted_iota(jnp.int32, (CHUNK, CHUNK), 0))
    for g in range(GMLP_GROUPS):
        wm = jnp.where(tril, ws_ref[g], 0.0).astype(BF16)
        bias = bt_ref[:, g:g + 1]
        for ch in range(tm // CHUNK):
            rows = slice(ch * CHUNK, (ch + 1) * CHUNK)
            cols = slice(g * GMLP_GROUP, (g + 1) * GMLP_GROUP)
            mixed = _dot(wm, v_ref[rows, cols]) + bias
            go_ref[rows, cols] = (u_ref[rows, cols] * mixed).astype(BF16)


def _mix1_prompt(hp, u, v16, pw16, ps, ws, b_t, *, seq, tm):
    halo_blocks = tm // _HALO
    row2 = lambda i: (i, 0)
    vmem = 24 * tm * POOL_WIDTH * 4 + 4 * 1024 * 1024
    return pl.pallas_call(
        functools.partial(_mix1_prompt_kernel, tm=tm), grid=(seq // tm,),
        in_specs=[pl.BlockSpec((tm, POOL_WIDTH), row2),
                  pl.BlockSpec((_HALO, POOL_WIDTH), lambda i: (jnp.maximum(i * halo_blocks - 1, 0), 0)),
                  pl.BlockSpec((tm, GMLP_WIDTH), row2),
                  pl.BlockSpec((tm, GMLP_WIDTH), row2),
                  pl.BlockSpec(pw16.shape, lambda i: (0, 0, 0)),
                  pl.BlockSpec(ps.shape, lambda i: (0, 0)),
                  pl.BlockSpec(ws.shape, lambda i: (0, 0, 0)),
                  pl.BlockSpec(b_t.shape, lambda i: (0, 0))],
        out_specs=(pl.BlockSpec((tm, POOL_WIDTH), row2), pl.BlockSpec((tm, GMLP_WIDTH), row2)),
        out_shape=(jax.ShapeDtypeStruct((seq, POOL_WIDTH), BF16), jax.ShapeDtypeStruct((seq, GMLP_WIDTH), BF16)),
        compiler_params=_cparams(("parallel",), vmem), name="mix1_prompt",
    )(hp, hp, u, v16, pw16, ps, ws, b_t)


def _mix1_sample_kernel(ws_ref, b_ref, st_ref, hp_ref, u_ref, v_ref, pw_ref, ps_ref, po_ref, go_ref, *, t_s, past):
    for gi, w in enumerate(POOL_WINDOWS):
        cols = slice(gi * POOL_GROUP, (gi + 1) * POOL_GROUP)
        tails = [None] * w
        tail = None
        for k in range(1, w):
            row = st_ref[POOL_BUF - k, :, cols]
            tail = row if tail is None else tail + row
            tails[k] = tail
        for t in range(t_s):
            wsum = None
            for tp in range(max(0, t - w + 1), t + 1):
                r = hp_ref[tp, :, cols]
                wsum = r if wsum is None else wsum + r
            nbuf = max(w - 1 - t, 0)
            if nbuf > 0:
                wsum = wsum + tails[nbuf]
            cnt = float(min(past + t + 1, w))
            d = (wsum / cnt - hp_ref[t, :, cols]).astype(BF16)
            po_ref[t, :, cols] = (_dot(d, pw_ref[gi]) * ps_ref[:, cols]).astype(BF16)
    for g in range(GMLP_GROUPS):
        cols = slice(g * GMLP_GROUP, (g + 1) * GMLP_GROUP)
        for t in range(t_s):
            mixed = None
            for j in range(t + 1):
                term = ws_ref[g, t, j] * v_ref[j, :, cols].astype(F32)
                mixed = term if mixed is None else mixed + term
            mixed = mixed + b_ref[g, t]
            go_ref[t, :, cols] = (u_ref[t, :, cols] * mixed).astype(BF16)


def _mix1_sample(ws_small, b_small, st_t, hp_t, u_t, v16_t, pw16, ps, *, t_s, past):
    b = hp_t.shape[1]
    full = lambda a: pl.BlockSpec(a.shape, lambda i, n=a.ndim: (0,) * n)
    smem = lambda a: pl.BlockSpec(a.shape, lambda i, n=a.ndim: (0,) * n, memory_space=pltpu.SMEM)
    args = (ws_small, b_small, st_t, hp_t, u_t, v16_t, pw16, ps)
    in_specs = [smem(ws_small), smem(b_small)] + [full(a) for a in args[2:]]
    vmem = 4 * (st_t.size * 4 + 3 * hp_t.size * 4) + 8 * 1024 * 1024
    return pl.pallas_call(
        functools.partial(_mix1_sample_kernel, t_s=t_s, past=past), grid=(1,),
        in_specs=in_specs,
        out_specs=(pl.BlockSpec((t_s, b, POOL_WIDTH), lambda i: (0, 0, 0)),
                   pl.BlockSpec((t_s, b, GMLP_WIDTH), lambda i: (0, 0, 0))),
        out_shape=(jax.ShapeDtypeStruct((t_s, b, POOL_WIDTH), BF16),
                   jax.ShapeDtypeStruct((t_s, b, GMLP_WIDTH), BF16)),
        compiler_params=_cparams(("arbitrary",), vmem), name="mix1_sample",
    )(*args)


def _pick_tile(n, prefs):
    for t in prefs:
        if n % t == 0:
            return t
    raise ValueError(f"no tile in {prefs} divides {n}")


def kernel(x_prompt, x_sample, cache_mla_ckv, cache_mla_kpe, cache_diff_k, cache_diff_v, state_pool, page_table, norm_gains, w_up, w_down, mla_diff_w_in, mla_q_norm, mla_w_uq, mla_kv_norm, mla_w_uk, mla_w_uv, diff_lambda, diff_subln, mla_diff_w_out, pool_gmlp_w_in, pool_w, pool_scale, gmlp_norm, gmlp_ws, gmlp_b, pool_gmlp_w_out):
    bp, seq, d = x_prompt.shape
    b, t_s, _ = x_sample.shape
    assert bp == 1, "one prompt sequence"
    n_pages = page_table.shape[1]
    page = cache_mla_ckv.shape[2]
    past = n_pages * page
    n = seq + b * t_s
    depth = norm_gains.shape[0]
    assert t_s <= 16 and (MLA_HEADS * t_s) % 16 == 0

    ns = b * t_s
    tm = _pick_tile(math.gcd(seq, ns), (256, 128))
    tm_mlp = _pick_tile(n, (512, 256, 128))
    tf = _pick_tile(w_up.shape[2], (1024, 512))
    tq_d = _pick_tile(seq, (128,))
    tk_d = _pick_tile(seq, (512, 256, 128))
    tq_m = _pick_tile(seq, (512, 256, 128))
    pages = _pick_tile(n_pages, (16, 8, 4, 2, 1))

    xs = (x_prompt.reshape(seq, d), x_sample.reshape(ns, d))
    wup16 = w_up.astype(BF16)
    wdn16 = w_down.astype(BF16)
    half = MLA_ROPE // 2
    inv = ROPE_THETA ** (-jnp.arange(half, dtype=F32) / half)
    inv_tab = jnp.tile(inv, LANES // half).reshape(1, LANES)

    outs = {}
    for i in range(depth):
        g = norm_gains[i]
        j = i // 2
        if i % 2 == 0:
            lam_init = 0.8 - 0.6 * math.exp(-0.3 * i)
            w_in = mla_diff_w_in[j]
            o_cq, o_ckv, o_kpe = 0, Q_LORA, Q_LORA + KV_LORA
            o_dq = o_kpe + MLA_ROPE
            o_dk = o_dq + DIFF_HEADS * 2 * DIFF_HD
            o_dv = o_dk + DIFF_KV_HEADS * 2 * DIFF_HD
            win16 = jnp.concatenate(
                [w_in[:, o_cq:o_ckv], w_in[:, o_ckv:o_kpe], w_in[:, o_dq:], w_in[:, o_kpe:o_dq],
                 jnp.zeros((d, LANES - MLA_ROPE), F32)], axis=1).astype(BF16)
            wuq = mla_w_uq[j].reshape(Q_LORA, MLA_HEADS, MLA_NOPE + MLA_ROPE)
            wuq16 = jnp.concatenate([wuq[:, :, :MLA_NOPE].reshape(Q_LORA, -1),
                                     wuq[:, :, MLA_NOPE:].reshape(Q_LORA, -1)], axis=1).astype(BF16)
            wuk_t = jnp.transpose(mla_w_uk[j], (1, 2, 0)).astype(BF16)
            wuv_t = jnp.transpose(mla_w_uv[j], (1, 0, 2)).astype(BF16)
            wuk_f = mla_w_uk[j].reshape(KV_LORA, MLA_HEADS * MLA_NOPE).astype(BF16)
            wuv_f = mla_w_uv[j].reshape(KV_LORA, MLA_HEADS * MLA_V).astype(BF16)
            if len(xs) == 1:
                xs = (xs[0][:seq], xs[0][seq:])
            (qf, kf, vf, qlat, qpe, ckv16s, kpe16s, ckv32, kpe32, dq16, dk32, dk16, dv32, dv16) = _proj0(
                xs[0], xs[1], g[0:1], win16, mla_q_norm[j].reshape(1, -1), wuq16, mla_kv_norm[j].reshape(1, -1),
                wuk_t, wuk_f, wuv_f, inv_tab, past=past, t_s=t_s, tm=tm)
            lam_p = diff_lambda[j]
            sub = diff_subln[j].reshape(1, -1)

            omla_p = _mla_prompt(qf, kf, vf, tq=tq_m, tk=tq_m)

            qlat_s = jnp.transpose(qlat.reshape(MLA_HEADS, b, t_s, KV_LORA), (1, 0, 2, 3)).reshape(b, MLA_HEADS * t_s, KV_LORA)
            qpe_s = jnp.transpose(qpe.reshape(MLA_HEADS, b, t_s, MLA_ROPE), (1, 0, 2, 3)).reshape(b, MLA_HEADS * t_s, MLA_ROPE)
            dq_r = dq16[seq:].reshape(b, t_s, DIFF_KV_HEADS, DIFF_REP, 2, DIFF_HD)
            dq_r = jnp.transpose(dq_r, (0, 2, 4, 3, 1, 5))
            zq = jnp.zeros_like(dq_r[:, :, 0])
            dq_s = jnp.stack([jnp.concatenate([dq_r[:, :, 0], zq], axis=-1),
                              jnp.concatenate([zq, dq_r[:, :, 1]], axis=-1)], axis=2)
            dq_s = dq_s.reshape(b, DIFF_KV_HEADS * 2 * DIFF_REP * t_s, 2 * DIFF_HD)
            npad = 16

            def new_rows(a):
                a = a.reshape(b, t_s, a.shape[-1])
                return jnp.pad(a, ((0, 0), (0, npad - t_s), (0, 0)))

            def interleaved(a):
                return a.reshape(a.shape[:-2] + (a.shape[-2] * DIFF_KV_HEADS, a.shape[-1] // DIFF_KV_HEADS))

            n_pool = cache_diff_k.shape[1]
            olat_s, od_s, od_p = _sample_attn(
                page_table, qlat_s, qpe_s, dq_s, new_rows(ckv16s), new_rows(kpe16s),
                interleaved(new_rows(dk16[seq:])), interleaved(new_rows(dv16[seq:])),
                lam_p, sub, dq16, dk16, dv16, cache_mla_ckv, jnp.swapaxes(cache_mla_kpe, 2, 3),
                cache_diff_k.reshape(cache_diff_k.shape[0], n_pool, page * DIFF_KV_HEADS, 2 * DIFF_HD),
                cache_diff_v.reshape(cache_diff_v.shape[0], n_pool, page * DIFF_KV_HEADS, DIFF_VD),
                seq=seq, layer=j, pages=pages, t_s=t_s, lam_init=lam_init, tq=tq_d, tk=tk_d)
            olat_t = jnp.transpose(olat_s.reshape(b, MLA_HEADS, t_s, KV_LORA), (1, 0, 2, 3)).reshape(MLA_HEADS, b * t_s, KV_LORA)
            omla_s = _head_proj(olat_t, wuv_t)
            od_s = jnp.transpose(od_s.reshape(b, DIFF_KV_HEADS, DIFF_REP, t_s, DIFF_VD), (0, 3, 1, 2, 4)).reshape(b * t_s, DIFF_HEADS * DIFF_VD)
            mixed = (omla_p, omla_s, od_p, od_s)
            w_out16 = mla_diff_w_out[j].astype(BF16)

            outs.setdefault("ckv_p", []).append(ckv32[:seq].reshape(bp, seq, KV_LORA))
            outs.setdefault("kpe_p", []).append(kpe32[:seq].reshape(bp, seq, MLA_ROPE))
            outs.setdefault("k_p", []).append(dk32[:seq].reshape(bp, seq, DIFF_KV_HEADS, 2 * DIFF_HD))
            outs.setdefault("v_p", []).append(dv32[:seq].reshape(bp, seq, DIFF_KV_HEADS, DIFF_VD))
            outs.setdefault("ckv_s", []).append(ckv32[seq:].reshape(b, t_s, KV_LORA))
            outs.setdefault("kpe_s", []).append(kpe32[seq:].reshape(b, t_s, MLA_ROPE))
            outs.setdefault("k_s", []).append(dk32[seq:].reshape(b, t_s, DIFF_KV_HEADS, 2 * DIFF_HD))
            outs.setdefault("v_s", []).append(dv32[seq:].reshape(b, t_s, DIFF_KV_HEADS, DIFF_VD))
        else:
            if len(xs) == 2:
                xs = (jnp.concatenate(xs, axis=0),)
            hp, u, v32, v16 = _proj1(xs[0], g[0:1], pool_gmlp_w_in[j].astype(BF16), gmlp_norm[j].reshape(1, -1),
                                     tm=tm)
            pw16 = pool_w[j].astype(BF16)
            ps = pool_scale[j].reshape(1, -1)
            tm1 = _pick_tile(seq, (256, 128))
            po_p, go_p = _mix1_prompt(hp, u, v16, pw16, ps, gmlp_ws[j], gmlp_b[j].T, seq=seq, tm=tm1)

            def steps_first(a):
                return jnp.transpose(a[seq:].reshape(b, t_s, a.shape[-1]), (1, 0, 2))

            def rows_first(a):
                return jnp.transpose(a, (1, 0, 2)).reshape(ns, a.shape[-1])

            ws_small = gmlp_ws[j][:, :t_s, :t_s].astype(BF16).astype(F32)
            po_s, go_s = _mix1_sample(ws_small, gmlp_b[j][:, :t_s], jnp.transpose(state_pool[j], (1, 0, 2)),
                                      steps_first(hp), steps_first(u), steps_first(v16), pw16, ps, t_s=t_s, past=past)
            mixed = (po_p, rows_first(po_s), go_p, rows_first(go_s))
            w_out16 = pool_gmlp_w_out[j].astype(BF16)

            hp_s = hp[seq:].reshape(b, t_s, POOL_WIDTH)
            outs.setdefault("pool_p", []).append(hp[seq - POOL_BUF:seq].reshape(bp, POOL_BUF, POOL_WIDTH))
            outs.setdefault("pool_s", []).append(jnp.concatenate([state_pool[j], hp_s], axis=1)[:, -POOL_BUF:])
            outs.setdefault("gv_s", []).append(v32[seq:].reshape(b, t_s, GMLP_WIDTH))

        x = _out_proj(*mixed, w_out16, g[1:2], xs, tm=tm)
        xs = (_mlp(x, g[2:3], g[3:4], wup16, wdn16, layer=i, tm=tm_mlp, tf=tf),)

    x = xs[0]
    st = lambda key: jnp.stack(outs[key])
    return (x[:seq].reshape(bp, seq, d), x[seq:].reshape(b, t_s, d),
            st("ckv_p"), st("kpe_p"), st("k_p"), st("v_p"), st("pool_p"),
            st("ckv_s"), st("kpe_s"), st("k_s"), st("v_s"), st("pool_s"), st("gv_s"))
```

```python
import functools
import math

import jax
import jax.numpy as jnp
from jax import lax
from jax.experimental import pallas as pl
from jax.experimental.pallas import tpu as pltpu

F32 = jnp.float32
BF16 = jnp.bfloat16

EPS = 1e-6
ROPE_THETA = 10000.0
MLA_HEADS = 8
MLA_NOPE = 128
MLA_ROPE = 64
MLA_V = 128
Q_LORA = 512
KV_LORA = 512
MLA_SCALE = (MLA_NOPE + MLA_ROPE) ** -0.5
DIFF_HEADS = 8
DIFF_KV_HEADS = 2
DIFF_REP = DIFF_HEADS // DIFF_KV_HEADS
DIFF_HD = 64
DIFF_VD = 2 * DIFF_HD
DIFF_SCALE = DIFF_HD ** -0.5
POOL_WINDOWS = (2, 4, 8, 16)
POOL_GROUP = 256
POOL_WIDTH = len(POOL_WINDOWS) * POOL_GROUP
POOL_BUF = max(POOL_WINDOWS) - 1
CHUNK = 128
GMLP_GROUPS = 4
GMLP_GROUP = 256
GMLP_WIDTH = GMLP_GROUPS * GMLP_GROUP

LOG2E = 1.4426950408889634
NEG_BIG = -1e30
LANES = 128
V7X_VMEM_BUDGET = 56 * 1024 * 1024

_Z_CQ = 0
_Z_CKV = _Z_CQ + Q_LORA
_Z_DQ = _Z_CKV + KV_LORA
_Z_DK = _Z_DQ + DIFF_HEADS * 2 * DIFF_HD
_Z_DV = _Z_DK + DIFF_KV_HEADS * 2 * DIFF_HD
_Z_KPE = _Z_DV + DIFF_KV_HEADS * DIFF_VD
_Z_END = _Z_KPE + LANES


def _cparams(sem, vmem_bytes):
    return pltpu.CompilerParams(dimension_semantics=sem,
                                vmem_limit_bytes=int(min(vmem_bytes, V7X_VMEM_BUDGET)))


def _rms(x, g):
    return x * lax.rsqrt(jnp.mean(x * x, axis=-1, keepdims=True) + EPS) * g


def _dot(a, b):
    return jnp.dot(a, b, preferred_element_type=F32)


def _dot_nt(a, b):
    return lax.dot_general(a, b, (((1,), (1,)), ((), ())), preferred_element_type=F32)


def _rope128(x, cos, sin_signed, first_half):
    swapped = jnp.where(first_half, pltpu.roll(x, LANES - 32, axis=1), pltpu.roll(x, 32, axis=1))
    return x * cos + swapped * sin_signed


def _row_specs(tm, width, n_p):
    return [pl.BlockSpec((tm, width), lambda i, *_: (jnp.minimum(i, n_p - 1), 0)),
            pl.BlockSpec((tm, width), lambda i, *_: (jnp.maximum(i - n_p, 0), 0))]


def _read_rows(i, n_p, p_ref, s_ref):
    return jnp.where(i < n_p, p_ref[...], s_ref[...])


_QK_W = 2 * LANES


def _proj0_kernel(xp_ref, xs_ref, g_ref, win_ref, qn_ref, wuq_ref, kvn_ref, wuk_ref, wukf_ref, wuvf_ref, inv_ref,
                  qf_ref, kf_ref, vf_ref, qlat_ref, qpe_ref, ckv16_ref, kpe16_ref,
                  ckv32_ref, kpe32_ref, dq_ref, dk32_ref, dk16_ref, dv32_ref, dv16_ref,
                  *, tm, n_p, seq, past, t_s):
    i = pl.program_id(0)
    h = _rms(_read_rows(i, n_p, xp_ref, xs_ref), g_ref[...]).astype(BF16)
    z = _dot(h, win_ref[...])

    row = lax.broadcasted_iota(jnp.int32, (tm, LANES), 0) + i * tm
    pos = jnp.where(row < seq, row, past + lax.rem(row - seq, t_s)).astype(F32)
    ang = pos * inv_ref[...]
    cos = jnp.cos(ang)
    sin = jnp.sin(ang)
    lane = lax.broadcasted_iota(jnp.int32, (tm, LANES), 1)
    first_half = (lane & 32) == 0
    sin_signed = jnp.where(first_half, -sin, sin)

    ckv = _rms(z[:, _Z_CKV:_Z_CKV + KV_LORA], kvn_ref[...])
    ckv32_ref[...] = ckv
    ckv16 = ckv.astype(BF16)
    kpe = _rope128(z[:, _Z_KPE:_Z_KPE + LANES], cos, sin_signed, first_half)[:, :MLA_ROPE]
    kpe32_ref[...] = kpe
    kpe16 = kpe.astype(BF16)

    dq_ref[...] = (z[:, _Z_DQ:_Z_DK] * DIFF_SCALE).astype(BF16)
    dk = z[:, _Z_DK:_Z_DV]
    dk32_ref[...] = dk
    dk16_ref[...] = dk.astype(BF16)
    dv = z[:, _Z_DV:_Z_KPE]
    dv32_ref[...] = dv
    dv16_ref[...] = dv.astype(BF16)

    cq = _rms(z[:, _Z_CQ:_Z_CQ + Q_LORA], qn_ref[...]).astype(BF16)
    q = _dot(cq, wuq_ref[...])
    pe0 = MLA_HEADS * MLA_NOPE
    q_pe = [_rope128(q[:, pe0 + c * LANES:pe0 + (c + 1) * LANES], cos, sin_signed, first_half).astype(BF16)
            for c in range(MLA_HEADS * MLA_ROPE // LANES)]

    @pl.when(i < n_p)
    def _():
        kn = _dot(ckv16, wukf_ref[...])
        vf_ref[...] = _dot(ckv16, wuvf_ref[...]).astype(BF16)
        zero = jnp.zeros((tm, _QK_W - MLA_NOPE - MLA_ROPE), BF16)
        for hd in range(MLA_HEADS):
            cols = slice(hd * MLA_NOPE, (hd + 1) * MLA_NOPE)
            half = slice((hd % 2) * MLA_ROPE, (hd % 2 + 1) * MLA_ROPE)
            kf_ref[hd, :, :MLA_NOPE] = kn[:, cols].astype(BF16)
            kf_ref[hd, :, MLA_NOPE:MLA_NOPE + MLA_ROPE] = kpe16
            kf_ref[hd, :, MLA_NOPE + MLA_ROPE:] = zero
            qf_ref[hd, :, :MLA_NOPE] = q[:, cols].astype(BF16)
            qf_ref[hd, :, MLA_NOPE:MLA_NOPE + MLA_ROPE] = q_pe[hd // 2][:, half]
            qf_ref[hd, :, MLA_NOPE + MLA_ROPE:] = zero

    @pl.when(i >= n_p)
    def _():
        ckv16_ref[...] = ckv16
        kpe16_ref[...] = kpe16
        for hd in range(MLA_HEADS):
            qn = q[:, hd * MLA_NOPE:(hd + 1) * MLA_NOPE].astype(BF16)
            qlat_ref[hd] = _dot(qn, wuk_ref[hd]).astype(BF16)
            qpe_ref[hd] = q_pe[hd // 2][:, (hd % 2) * MLA_ROPE:(hd % 2 + 1) * MLA_ROPE]


def _proj0(xp, xs, g, win, qn, wuq, kvn, wuk_t, wuk_f, wuv_f, inv, *, past, t_s, tm):
    seq, d = xp.shape
    ns = xs.shape[0]
    n = seq + ns
    n_p = seq // tm
    const2 = lambda i: (0, 0)
    row2 = lambda i: (i, 0)
    prow3 = lambda i: (0, jnp.minimum(i, n_p - 1), 0)
    srow3 = lambda i: (0, jnp.maximum(i - n_p, 0), 0)
    prow2 = lambda i: (jnp.minimum(i, n_p - 1), 0)
    srow2 = lambda i: (jnp.maximum(i - n_p, 0), 0)
    kw = DIFF_KV_HEADS * 2 * DIFF_HD
    vw = DIFF_KV_HEADS * DIFF_VD
    out_shape = (
        jax.ShapeDtypeStruct((MLA_HEADS, seq, _QK_W), BF16),
        jax.ShapeDtypeStruct((MLA_HEADS, seq, _QK_W), BF16),
        jax.ShapeDtypeStruct((seq, MLA_HEADS * MLA_V), BF16),
        jax.ShapeDtypeStruct((MLA_HEADS, ns, KV_LORA), BF16),
        jax.ShapeDtypeStruct((MLA_HEADS, ns, MLA_ROPE), BF16),
        jax.ShapeDtypeStruct((ns, KV_LORA), BF16),
        jax.ShapeDtypeStruct((ns, MLA_ROPE), BF16),
        jax.ShapeDtypeStruct((n, KV_LORA), F32),
        jax.ShapeDtypeStruct((n, MLA_ROPE), F32),
        jax.ShapeDtypeStruct((n, DIFF_HEADS * 2 * DIFF_HD), BF16),
        jax.ShapeDtypeStruct((n, kw), F32),
        jax.ShapeDtypeStruct((n, kw), BF16),
        jax.ShapeDtypeStruct((n, vw), F32),
        jax.ShapeDtypeStruct((n, vw), BF16),
    )
    out_specs = (
        pl.BlockSpec((MLA_HEADS, tm, _QK_W), prow3),
        pl.BlockSpec((MLA_HEADS, tm, _QK_W), prow3),
        pl.BlockSpec((tm, MLA_HEADS * MLA_V), prow2),
        pl.BlockSpec((MLA_HEADS, tm, KV_LORA), srow3),
        pl.BlockSpec((MLA_HEADS, tm, MLA_ROPE), srow3),
        pl.BlockSpec((tm, KV_LORA), srow2),
        pl.BlockSpec((tm, MLA_ROPE), srow2),
        pl.BlockSpec((tm, KV_LORA), row2),
        pl.BlockSpec((tm, MLA_ROPE), row2),
        pl.BlockSpec((tm, DIFF_HEADS * 2 * DIFF_HD), row2),
        pl.BlockSpec((tm, kw), row2),
        pl.BlockSpec((tm, kw), row2),
        pl.BlockSpec((tm, vw), row2),
        pl.BlockSpec((tm, vw), row2),
    )
    in_specs = _row_specs(tm, d, n_p) + [
        pl.BlockSpec((1, d), const2),
        pl.BlockSpec(win.shape, const2),
        pl.BlockSpec((1, Q_LORA), const2),
        pl.BlockSpec(wuq.shape, const2),
        pl.BlockSpec((1, KV_LORA), const2),
        pl.BlockSpec(wuk_t.shape, lambda i: (0, 0, 0)),
        pl.BlockSpec(wuk_f.shape, const2),
        pl.BlockSpec(wuv_f.shape, const2),
        pl.BlockSpec((1, LANES), const2),
    ]
    weights = win.size + wuq.size + wuk_t.size + wuk_f.size + wuv_f.size
    vmem = 2 * weights * 2 + 4 * tm * d * 4 + 16 * tm * _Z_END * 4
    return pl.pallas_call(
        functools.partial(_proj0_kernel, tm=tm, n_p=n_p, seq=seq, past=past, t_s=t_s),
        grid=(n // tm,), in_specs=in_specs, out_specs=out_specs, out_shape=out_shape,
        compiler_params=_cparams(("arbitrary",), vmem), name="proj0",
    )(xp, xs, g, win, qn, wuq, kvn, wuk_t, wuk_f, wuv_f, inv)


def _lanes_to(x, width):
    if width <= LANES:
        return x[:, :width]
    return jnp.concatenate([x] * (width // LANES), axis=1)


def _softmax_update(s2, v16, m_ref, l_ref, acc_ref):
    m_prev = m_ref[...]
    m_new = jnp.maximum(m_prev, jnp.max(s2, axis=-1, keepdims=True))
    p = jnp.exp2(s2 - _lanes_to(m_new, s2.shape[1]))
    alpha = jnp.exp2(m_prev - m_new)
    l_ref[...] = alpha * l_ref[...] + jnp.sum(p, axis=-1, keepdims=True)
    acc_ref[...] = _lanes_to(alpha, acc_ref.shape[-1]) * acc_ref[...] + _dot(p.astype(BF16), v16)
    m_ref[...] = m_new


def _softmax_finish(acc_ref, l_ref):
    return acc_ref[...] / _lanes_to(l_ref[...], acc_ref.shape[-1])


def _diff_lambda(lam_ref, lam_init):
    lp = lam_ref[...]
    a = jnp.sum(lp[0:1] * lp[1:2], axis=-1, keepdims=True)
    b = jnp.sum(lp[2:3] * lp[3:4], axis=-1, keepdims=True)
    return jnp.exp(a) - jnp.exp(b) + lam_init


_HEADS_PER_STEP = 4


def _mla_prompt_kernel(q_ref, k_ref, v_ref, o_ref, m_ref, l_ref, acc_ref, *, tq, tk):
    i = pl.program_id(1)
    m_ref[...] = jnp.full(m_ref.shape, NEG_BIG, F32)
    l_ref[...] = jnp.zeros(l_ref.shape, F32)
    acc_ref[...] = jnp.zeros(acc_ref.shape, F32)
    c = MLA_SCALE * LOG2E

    def block(j, masked):
        k0 = pl.multiple_of(j * tk, tk)
        s2s = [_dot_nt(q_ref[hh], k_ref[hh, pl.ds(k0, tk), :]) * c for hh in range(_HEADS_PER_STEP)]
        if masked:
            qpos = i * tq + lax.broadcasted_iota(jnp.int32, (tq, tk), 0)
            kpos = j * tk + lax.broadcasted_iota(jnp.int32, (tq, tk), 1)
            s2s = [jnp.where(kpos <= qpos, s2, NEG_BIG) for s2 in s2s]
        for hh, s2 in enumerate(s2s):
            _softmax_update(s2, v_ref[pl.ds(k0, tk), hh * MLA_V:(hh + 1) * MLA_V],
                            m_ref.at[hh], l_ref.at[hh], acc_ref.at[hh])

    def full_block(j, carry):
        block(j, False)
        return carry

    jd = (i * tq) // tk
    lax.fori_loop(0, jd, full_block, 0)
    block(jd, True)
    for hh in range(_HEADS_PER_STEP):
        o_ref[:, hh * MLA_V:(hh + 1) * MLA_V] = _softmax_finish(acc_ref.at[hh], l_ref.at[hh]).astype(BF16)


def _mla_prompt(qf, kf, vf, *, tq, tk):
    heads, seq, w = qf.shape
    hps = _HEADS_PER_STEP
    assert heads % hps == 0 and tk % tq == 0
    vmem = (hps * seq * (w + MLA_V) * 2 + 4 * hps * tq * (w + MLA_V) * 2
            + hps * tq * (2 * LANES + MLA_V) * 4 + 6 * hps * tq * tk * 4)
    return pl.pallas_call(
        functools.partial(_mla_prompt_kernel, tq=tq, tk=tk),
        grid=(heads // hps, seq // tq),
        in_specs=[
            pl.BlockSpec((hps, tq, w), lambda hp, i: (hp, i, 0)),
            pl.BlockSpec((hps, seq, w), lambda hp, i: (hp, 0, 0), pipeline_mode=pl.Buffered(1)),
            pl.BlockSpec((seq, hps * MLA_V), lambda hp, i: (0, hp), pipeline_mode=pl.Buffered(1)),
        ],
        out_specs=pl.BlockSpec((tq, hps * MLA_V), lambda hp, i: (i, hp)),
        out_shape=jax.ShapeDtypeStruct((seq, heads * MLA_V), BF16),
        scratch_shapes=[pltpu.VMEM((hps, tq, LANES), F32), pltpu.VMEM((hps, tq, LANES), F32),
                        pltpu.VMEM((hps, tq, MLA_V), F32)],
        compiler_params=_cparams(("parallel", "parallel"), vmem), name="mla_prompt",
    )(qf, kf, vf)


def _diff_finish(a, lam, sub, scale_out):
    half = a.shape[0] // 2
    o = a[:half] - lam * a[half:]
    return _rms(o, sub) * scale_out


_ITEM_GROUP = 1
_ITEM_DIAG = 2
_ITEM_FIRST = 4


def _diff_prompt_schedule(seq, tq, tk, total_steps):
    nq = seq // tq
    counts = [DIFF_KV_HEADS * ((q * tq) // tk + 1) for q in range(nq)]
    max_items = -(-sum(counts) // total_steps)
    steps_q = [-(-c // max_items) for c in counts]
    spare = total_steps - sum(steps_q)
    assert spare >= 0
    while spare > 0:
        q = max(range(nq), key=lambda t: counts[t] / steps_q[t])
        if steps_q[q] >= counts[q]:
            break
        steps_q[q] += 1
        spare -= 1
    step_start, step_q, item_j, item_fl = [0], [], [], []
    for q in range(nq):
        jd = (q * tq) // tk
        base = len(item_j)
        for j in range(jd + 1):
            for g in range(DIFF_KV_HEADS):
                item_j.append(j)
                item_fl.append(g * _ITEM_GROUP + (_ITEM_DIAG if j == jd else 0)
                               + (_ITEM_FIRST if j == 0 and g == 0 else 0))
        for k in range(steps_q[q]):
            step_q.append(q)
            step_start.append(base + ((k + 1) * counts[q]) // steps_q[q])
    while len(step_q) < total_steps:
        step_q.append(nq - 1)
        step_start.append(len(item_j))
    as_i32 = lambda a: jnp.asarray(a, dtype=jnp.int32)
    return as_i32(step_start), as_i32(step_q), as_i32(item_j), as_i32(item_fl), max_items


def _diff_prompt_start(dq_ref, q_scr, m_ref, l_ref, acc_ref, tq):
    lane = lax.broadcasted_iota(jnp.int32, (tq, 2 * DIFF_HD), 1)
    for g in range(DIFF_KV_HEADS):
        for r in range(DIFF_REP):
            hd = g * DIFF_REP + r
            qh = dq_ref[:, hd * 2 * DIFF_HD:(hd + 1) * 2 * DIFF_HD]
            zero = jnp.zeros_like(qh)
            q_scr[g, r * tq:(r + 1) * tq, :] = jnp.where(lane < DIFF_HD, qh, zero)
            q_scr[g, (DIFF_REP + r) * tq:(DIFF_REP + r + 1) * tq, :] = jnp.where(lane < DIFF_HD, zero, qh)
    m_ref[...] = jnp.full(m_ref.shape, NEG_BIG, F32)
    l_ref[...] = jnp.zeros(l_ref.shape, F32)
    acc_ref[...] = jnp.zeros(acc_ref.shape, F32)


def _diff_prompt_item(q, j, g, diag, q_scr, dk_ref, dv_ref, m_ref, l_ref, acc_ref, lam_ref, sub_ref, o_ref,
                      *, tq, tk, lam_init):
    rows = 2 * DIFF_REP * tq
    k0 = pl.multiple_of(j * tk, tk)
    s2 = _dot_nt(q_scr[g], dk_ref[pl.ds(k0, tk), g * 2 * DIFF_HD:(g + 1) * 2 * DIFF_HD]) * LOG2E
    if diag:
        qpos = q * tq + (lax.broadcasted_iota(jnp.int32, (rows, tk), 0) & (tq - 1))
        kpos = j * tk + lax.broadcasted_iota(jnp.int32, (rows, tk), 1)
        s2 = jnp.where(kpos <= qpos, s2, NEG_BIG)
    _softmax_update(s2, dv_ref[pl.ds(k0, tk), g * DIFF_VD:(g + 1) * DIFF_VD],
                    m_ref.at[g], l_ref.at[g], acc_ref.at[g])
    if diag:
        lam = _diff_lambda(lam_ref, lam_init)
        o = _diff_finish(_softmax_finish(acc_ref.at[g], l_ref.at[g]), lam, sub_ref[...], 1.0 - lam_init)
        for r in range(DIFF_REP):
            hd = g * DIFF_REP + r
            o_ref[:, hd * DIFF_VD:(hd + 1) * DIFF_VD] = o[r * tq:(r + 1) * tq].astype(BF16)


_N_CACHES = 4
_PAGE_SLOTS = 3


def _sample_attn_kernel(pt_ref, sstart_ref, sq_ref, ij_ref, ifl_ref,
                        qlat_ref, qpe_ref, dq_ref, ckvn_ref, kpen_ref, dkn_ref, dvn_ref,
                        lam_ref, sub_ref, pdq_ref, pdk_ref, pdv_ref, ckv_hbm, kpe_hbm, dk_hbm, dv_hbm,
                        olat_ref, od_ref, pod_ref,
                        ckv_buf, kpe_buf, dk_buf, dv_buf, sems,
                        m1_ref, l1_ref, acc1_ref, m2_ref, l2_ref, acc2_ref,
                        pq_scr, pm_ref, pl_ref, pacc_ref,
                        *, pages, layer, t_s, lam_init, tq, tk, max_items):
    steps = pl.num_programs(1)
    s = pl.program_id(1)
    n = pl.program_id(0) * steps + s
    total = pl.num_programs(0) * steps
    slot = lax.rem(n, _PAGE_SLOTS)

    def page_copies(issue):
        step = jnp.minimum(issue, total - 1)
        slot_ = issue % _PAGE_SLOTS if isinstance(issue, int) else lax.rem(issue, _PAGE_SLOTS)
        cps = []
        for kk in range(pages):
            pid = pt_ref[step * pages + kk]
            for ci, (hbm, buf) in enumerate(((ckv_hbm, ckv_buf), (kpe_hbm, kpe_buf),
                                             (dk_hbm, dk_buf), (dv_hbm, dv_buf))):
                cps.append(pltpu.make_async_copy(hbm.at[layer, pid], buf.at[slot_, kk], sems.at[slot_, ci]))
        return cps

    @pl.when(n == 0)
    def _():
        for issue in range(_PAGE_SLOTS - 1):
            for cp in page_copies(issue):
                cp.start()

    ql = qlat_ref[...]
    qp = qpe_ref[...]
    dq = dq_ref[...]
    c = MLA_SCALE * LOG2E
    rows1 = ql.shape[0]
    rows2 = dq.shape[0]
    grows = rows2 // DIFF_KV_HEADS

    def group_mask(keys):
        grp = lax.broadcasted_iota(jnp.int32, (rows2, keys), 0) // grows
        return (lax.broadcasted_iota(jnp.int32, (rows2, keys), 1) & (DIFF_KV_HEADS - 1)) == grp

    @pl.when(s == 0)
    def _():
        m1_ref[...] = jnp.full(m1_ref.shape, NEG_BIG, F32)
        l1_ref[...] = jnp.zeros(l1_ref.shape, F32)
        acc1_ref[...] = jnp.zeros(acc1_ref.shape, F32)
        m2_ref[...] = jnp.full(m2_ref.shape, NEG_BIG, F32)
        l2_ref[...] = jnp.zeros(l2_ref.shape, F32)
        acc2_ref[...] = jnp.zeros(acc2_ref.shape, F32)
        kn = ckvn_ref[...]
        npad = kn.shape[0]
        s2 = (_dot_nt(ql, kn) + _dot_nt(qp, kpen_ref[...])) * c
        step1 = lax.rem(lax.broadcasted_iota(jnp.int32, (rows1, npad), 0), t_s)
        key1 = lax.broadcasted_iota(jnp.int32, (rows1, npad), 1)
        _softmax_update(jnp.where(key1 <= step1, s2, NEG_BIG), kn, m1_ref, l1_ref, acc1_ref)
        dkn = dkn_ref[...]
        nk = dkn.shape[0]
        sd = _dot_nt(dq, dkn) * LOG2E
        step2 = lax.rem(lax.broadcasted_iota(jnp.int32, (rows2, nk), 0), t_s)
        key2 = lax.broadcasted_iota(jnp.int32, (rows2, nk), 1) // DIFF_KV_HEADS
        ok = jnp.logical_and(group_mask(nk), key2 <= step2)
        _softmax_update(jnp.where(ok, sd, NEG_BIG), dvn_ref[...], m2_ref, l2_ref, acc2_ref)

    for cp in page_copies(n):
        cp.wait()

    k = jnp.concatenate([ckv_buf[slot, kk].astype(BF16) for kk in range(pages)], axis=0)
    kp_t = jnp.concatenate([kpe_buf[slot, kk].astype(BF16) for kk in range(pages)], axis=1)
    dk = jnp.concatenate([dk_buf[slot, kk].astype(BF16) for kk in range(pages)], axis=0)
    dv = jnp.concatenate([dv_buf[slot, kk].astype(BF16) for kk in range(pages)], axis=0)
    s2 = (_dot_nt(ql, k) + _dot(qp, kp_t)) * c
    sd = jnp.where(group_mask(dk.shape[0]), _dot_nt(dq, dk) * LOG2E, NEG_BIG)
    _softmax_update(s2, k, m1_ref, l1_ref, acc1_ref)
    _softmax_update(sd, dv, m2_ref, l2_ref, acc2_ref)

    for cp in page_copies(n + _PAGE_SLOTS - 1):
        cp.start()

    @pl.when(n == total - 1)
    def _():
        for issue in range(1, _PAGE_SLOTS):
            for cp in page_copies(n + issue):
                cp.wait()

    @pl.when(s == steps - 1)
    def _():
        olat_ref[...] = _softmax_finish(acc1_ref, l1_ref).astype(BF16)
        lam = _diff_lambda(lam_ref, lam_init)
        a = _softmax_finish(acc2_ref, l2_ref)
        for g in range(DIFF_KV_HEADS):
            od_ref[g] = _diff_finish(a[g * grows:(g + 1) * grows], lam, sub_ref[...], 1.0 - lam_init).astype(BF16)

    q_tile = sq_ref[n]
    item0 = sstart_ref[n]
    item_end = sstart_ref[n + 1]
    for r in range(max_items):
        item = item0 + r

        @pl.when(item < item_end)
        def _():
            j = ij_ref[item]
            fl = ifl_ref[item]

            @pl.when((fl & _ITEM_FIRST) != 0)
            def _():
                _diff_prompt_start(pdq_ref, pq_scr, pm_ref, pl_ref, pacc_ref, tq)

            for g in range(DIFF_KV_HEADS):
                for diag in (False, True):
                    @pl.when((fl & (_ITEM_GROUP | _ITEM_DIAG)) == g * _ITEM_GROUP + (_ITEM_DIAG if diag else 0))
                    def _():
                        _diff_prompt_item(q_tile, j, g, diag, pq_scr, pdk_ref, pdv_ref, pm_ref, pl_ref, pacc_ref,
                                          lam_ref, sub_ref, pod_ref, tq=tq, tk=tk, lam_init=lam_init)


def _sample_attn(page_table, qlat_s, qpe_s, dq_s, ckvn, kpen, dkn, dvn, lam_p, sub, pdq, pdk, pdv,
                 cache_ckv, cache_kpe, cache_k, cache_v, *, seq, layer, pages, t_s, lam_init, tq, tk):
    b, n_pages = page_table.shape
    assert n_pages % pages == 0 and tk % tq == 0
    steps = n_pages // pages
    rows1 = qlat_s.shape[1]
    rows2 = dq_s.shape[1]
    grows = rows2 // DIFF_KV_HEADS
    prows = 2 * DIFF_REP * tq
    sstart, sq, ij, ifl, max_items = _diff_prompt_schedule(seq, tq, tk, b * steps)

    def seq3(a):
        return pl.BlockSpec((None,) + a.shape[1:], lambda bi, si, *_: (bi, 0, 0))

    const2 = lambda bi, si, *_: (0, 0)
    tile2 = lambda bi, si, pt, ss, sq_, *_: (sq_[bi * steps + si], 0)
    hbm = pl.BlockSpec(memory_space=pl.ANY)
    in_specs = [
        seq3(qlat_s), seq3(qpe_s), seq3(dq_s), seq3(ckvn), seq3(kpen), seq3(dkn), seq3(dvn),
        pl.BlockSpec(lam_p.shape, const2),
        pl.BlockSpec(sub.shape, const2),
        pl.BlockSpec((tq, pdq.shape[1]), tile2),
        pl.BlockSpec((seq, pdk.shape[1]), const2, pipeline_mode=pl.Buffered(1)),
        pl.BlockSpec((seq, pdv.shape[1]), const2, pipeline_mode=pl.Buffered(1)),
        hbm, hbm, hbm, hbm,
    ]
    out_specs = (
        pl.BlockSpec((None, rows1, KV_LORA), lambda bi, si, *_: (bi, 0, 0)),
        pl.BlockSpec((None, DIFF_KV_HEADS, grows // 2, DIFF_VD), lambda bi, si, *_: (bi, 0, 0, 0)),
        pl.BlockSpec((tq, DIFF_HEADS * DIFF_VD), tile2),
    )
    out_shape = (
        jax.ShapeDtypeStruct((b, rows1, KV_LORA), BF16),
        jax.ShapeDtypeStruct((b, DIFF_KV_HEADS, grows // 2, DIFF_VD), BF16),
        jax.ShapeDtypeStruct((seq, DIFF_HEADS * DIFF_VD), BF16),
    )
    bufs = [pltpu.VMEM((_PAGE_SLOTS, pages) + c.shape[2:], c.dtype)
            for c in (cache_ckv, cache_kpe, cache_k, cache_v)]
    scratch = bufs + [
        pltpu.SemaphoreType.DMA((_PAGE_SLOTS, _N_CACHES)),
        pltpu.VMEM((rows1, LANES), F32), pltpu.VMEM((rows1, LANES), F32), pltpu.VMEM((rows1, KV_LORA), F32),
        pltpu.VMEM((rows2, LANES), F32), pltpu.VMEM((rows2, LANES), F32), pltpu.VMEM((rows2, DIFF_VD), F32),
        pltpu.VMEM((DIFF_KV_HEADS, prows, 2 * DIFF_HD), BF16),
        pltpu.VMEM((DIFF_KV_HEADS, prows, LANES), F32),
        pltpu.VMEM((DIFF_KV_HEADS, prows, LANES), F32),
        pltpu.VMEM((DIFF_KV_HEADS, prows, DIFF_VD), F32)]
    step_bytes = pages * sum(math.prod(c.shape[2:]) for c in (cache_ckv, cache_kpe, cache_k, cache_v)) * 4
    prompt_bytes = (seq * (pdk.shape[1] + pdv.shape[1]) * 2 + 8 * tq * pdq.shape[1] * 2
                    + DIFF_KV_HEADS * prows * (2 * DIFF_HD * 2 + (2 * LANES + DIFF_VD) * 4) + 5 * prows * tk * 4)
    vmem = _PAGE_SLOTS * step_bytes + 2 * step_bytes + 6 * 1024 * 1024 + prompt_bytes
    grid_spec = pltpu.PrefetchScalarGridSpec(
        num_scalar_prefetch=5, grid=(b, steps), in_specs=in_specs, out_specs=out_specs,
        scratch_shapes=scratch)
    return pl.pallas_call(
        functools.partial(_sample_attn_kernel, pages=pages, layer=layer, t_s=t_s, lam_init=lam_init,
                          tq=tq, tk=tk, max_items=max_items),
        grid_spec=grid_spec, out_shape=out_shape,
        compiler_params=_cparams(("arbitrary", "arbitrary"), vmem), name="sample_attn",
    )(page_table.reshape(-1), sstart, sq, ij, ifl, qlat_s, qpe_s, dq_s, ckvn, kpen, dkn, dvn, lam_p, sub,
      pdq, pdk, pdv, cache_ckv, cache_kpe, cache_k, cache_v)


def _head_proj_kernel(o_ref, w_ref, out_ref):
    out_ref[...] = _dot(o_ref[...], w_ref[...]).astype(BF16)


def _head_proj(o_t, wuv_t):
    heads, rows, r = o_t.shape
    return pl.pallas_call(
        _head_proj_kernel, grid=(heads,),
        in_specs=[pl.BlockSpec((None, rows, r), lambda hd: (hd, 0, 0)),
                  pl.BlockSpec((None, r, MLA_V), lambda hd: (hd, 0, 0))],
        out_specs=pl.BlockSpec((rows, MLA_V), lambda hd: (0, hd)),
        out_shape=jax.ShapeDtypeStruct((rows, heads * MLA_V), BF16),
        compiler_params=_cparams(("parallel",), 16 * 1024 * 1024), name="head_proj",
    )(o_t, wuv_t)


def _out_proj_kernel(ap_ref, as_ref, bp_ref, bs_ref, w_ref, g_ref, *rest, n_p):
    *x_refs, o_ref = rest
    i = pl.program_id(0)
    ka = ap_ref.shape[1]
    mix = (_dot(_read_rows(i, n_p, ap_ref, as_ref), w_ref[:ka, :])
           + _dot(_read_rows(i, n_p, bp_ref, bs_ref), w_ref[ka:, :]))
    x = x_refs[0][...] if len(x_refs) == 1 else _read_rows(i, n_p, *x_refs)
    o_ref[...] = x + _rms(mix, g_ref[...])


def _out_proj(a_p, a_s, b_p, b_s, w16, g, xs, *, tm):
    seq, ka = a_p.shape
    kb = b_p.shape[1]
    n = seq + a_s.shape[0]
    d = w16.shape[1]
    n_p = seq // tm
    x_specs = [pl.BlockSpec((tm, d), lambda i: (i, 0))] if len(xs) == 1 else _row_specs(tm, d, n_p)
    vmem = 2 * (ka + kb) * d * 2 + 4 * tm * (ka + kb) * 2 + 12 * tm * d * 4
    return pl.pallas_call(
        functools.partial(_out_proj_kernel, n_p=n_p), grid=(n // tm,),
        in_specs=(_row_specs(tm, ka, n_p) + _row_specs(tm, kb, n_p)
                  + [pl.BlockSpec((ka + kb, d), lambda i: (0, 0)), pl.BlockSpec((1, d), lambda i: (0, 0))]
                  + x_specs),
        out_specs=pl.BlockSpec((tm, d), lambda i: (i, 0)),
        out_shape=jax.ShapeDtypeStruct((n, d), F32),
        compiler_params=_cparams(("parallel",), vmem), name="out_proj",
    )(a_p, a_s, b_p, b_s, w16, g, *xs)


def _mlp_kernel(x_ref, gpre_ref, gpost_ref, wup_ref, wdn_ref, o_ref, h_scr, acc_scr):
    k = pl.program_id(1)

    @pl.when(k == 0)
    def _():
        h_scr[...] = _rms(x_ref[...], gpre_ref[...]).astype(BF16)
        acc_scr[...] = jnp.zeros(acc_scr.shape, F32)

    a = jnp.maximum(_dot(h_scr[...], wup_ref[...]), 0.0)
    acc_scr[...] += _dot((a * a).astype(BF16), wdn_ref[...])

    @pl.when(k == pl.num_programs(1) - 1)
    def _():
        o_ref[...] = x_ref[...] + _rms(acc_scr[...], gpost_ref[...])


def _mlp(x, gpre, gpost, wup16, wdn16, *, layer, tm, tf):
    n, d = x.shape
    f = wup16.shape[2]
    vmem = 2 * 2 * d * tf * 2 + 4 * tm * d * 4 + tm * d * 2 + tm * d * 4 + 3 * tm * tf * 4 + 2 * tm * d * 4
    return pl.pallas_call(
        _mlp_kernel, grid=(n // tm, f // tf),
        in_specs=[pl.BlockSpec((tm, d), lambda i, k: (i, 0)),
                  pl.BlockSpec((1, d), lambda i, k: (0, 0)),
                  pl.BlockSpec((1, d), lambda i, k: (0, 0)),
                  pl.BlockSpec((None, d, tf), lambda i, k: (layer, 0, k)),
                  pl.BlockSpec((None, tf, d), lambda i, k: (layer, k, 0))],
        out_specs=pl.BlockSpec((tm, d), lambda i, k: (i, 0)),
        out_shape=jax.ShapeDtypeStruct((n, d), F32),
        scratch_shapes=[pltpu.VMEM((tm, d), BF16), pltpu.VMEM((tm, d), F32)],
        compiler_params=_cparams(("parallel", "arbitrary"), vmem), name="mlp",
    )(x, gpre, gpost, wup16, wdn16)


def _proj1_kernel(x_ref, g_ref, win_ref, gn_ref, hp_ref, u_ref, v32_ref, v16_ref):
    h = _rms(x_ref[...], g_ref[...]).astype(BF16)
    z = _dot(h, win_ref[...])
    hp_ref[...] = z[:, :POOL_WIDTH]
    uv = jax.nn.gelu(z[:, POOL_WIDTH:])
    u_ref[...] = uv[:, :GMLP_WIDTH]
    vv = uv[:, GMLP_WIDTH:]
    xc = vv - jnp.mean(vv, axis=-1, keepdims=True)
    v = xc * lax.rsqrt(jnp.mean(xc * xc, axis=-1, keepdims=True) + EPS) * gn_ref[...]
    v32_ref[...] = v
    v16_ref[...] = v.astype(BF16)


def _proj1(x, g, win16, gn, *, tm):
    n, d = x.shape
    w = win16.shape[1]
    row2 = lambda i: (i, 0)
    const2 = lambda i: (0, 0)
    vmem = 2 * win16.size * 2 + 2 * tm * d * 4 + 10 * tm * w * 4
    return pl.pallas_call(
        _proj1_kernel, grid=(n // tm,),
        in_specs=[pl.BlockSpec((tm, d), row2), pl.BlockSpec((1, d), const2),
                  pl.BlockSpec(win16.shape, const2), pl.BlockSpec((1, GMLP_WIDTH), const2)],
        out_specs=(pl.BlockSpec((tm, POOL_WIDTH), row2), pl.BlockSpec((tm, GMLP_WIDTH), row2),
                   pl.BlockSpec((tm, GMLP_WIDTH), row2), pl.BlockSpec((tm, GMLP_WIDTH), row2)),
        out_shape=(jax.ShapeDtypeStruct((n, POOL_WIDTH), F32), jax.ShapeDtypeStruct((n, GMLP_WIDTH), F32),
                   jax.ShapeDtypeStruct((n, GMLP_WIDTH), F32), jax.ShapeDtypeStruct((n, GMLP_WIDTH), BF16)),
        compiler_params=_cparams(("parallel",), vmem), name="proj1",
    )(x, g, win16, gn)


_HALO = 16


def _mix1_prompt_kernel(hp_ref, halo_ref, u_ref, v_ref, pw_ref, ps_ref, ws_ref, bt_ref, po_ref, go_ref, *, tm):
    i = pl.program_id(0)
    halo = jnp.where(i == 0, 0.0, halo_ref[...])
    ext = jnp.concatenate([halo, hp_ref[...]], axis=0)
    pos = i * tm + lax.broadcasted_iota(jnp.int32, (tm, POOL_GROUP), 0)
    for gi, w in enumerate(POOL_WINDOWS):
        run = ext[:, gi * POOL_GROUP:(gi + 1) * POOL_GROUP]
        width = 1
        while width < w:
            run = run + pltpu.roll(run, width, axis=0)
            width *= 2
        wsum = run[_HALO:]
        cnt = jnp.minimum(pos + 1, w).astype(F32)
        d = (wsum / cnt - hp_ref[:, gi * POOL_GROUP:(gi + 1) * POOL_GROUP]).astype(BF16)
        po = _dot(d, pw_ref[gi]) * ps_ref[:, gi * POOL_GROUP:(gi + 1) * POOL_GROUP]
        po_ref[:, gi * POOL_GROUP:(gi + 1) * POOL_GROUP] = po.astype(BF16)

    tril =(lax.broadcasted_iota(jnp.int32, (CHUNK, CHUNK), 1)
            <= lax.broadcasted_iota(jnp.int32, (CHUNK, CHUNK), 0))
    for g in range(GMLP_GROUPS):
        wm = jnp.where(tril, ws_ref[g], 0.0).astype(BF16)
        bias = bt_ref[:, g:g + 1]
        for ch in range(tm // CHUNK):
            rows = slice(ch * CHUNK, (ch + 1) * CHUNK)
            cols = slice(g * GMLP_GROUP, (g + 1) * GMLP_GROUP)
            mixed = _dot(wm, v_ref[rows, cols]) + bias
            go_ref[rows, cols] = (u_ref[rows, cols] * mixed).astype(BF16)


def _mix1_prompt(hp, u, v16, pw16, ps, ws, b_t, *, seq, tm):
    halo_blocks = tm // _HALO
    row2 = lambda i: (i, 0)
    vmem = 24 * tm * POOL_WIDTH * 4 + 4 * 1024 * 1024
    return pl.pallas_call(
        functools.partial(_mix1_prompt_kernel, tm=tm), grid=(seq // tm,),
        in_specs=[pl.BlockSpec((tm, POOL_WIDTH), row2),
                  pl.BlockSpec((_HALO, POOL_WIDTH), lambda i: (jnp.maximum(i * halo_blocks - 1, 0), 0)),
                  pl.BlockSpec((tm, GMLP_WIDTH), row2),
                  pl.BlockSpec((tm, GMLP_WIDTH), row2),
                  pl.BlockSpec(pw16.shape, lambda i: (0, 0, 0)),
                  pl.BlockSpec(ps.shape, lambda i: (0, 0)),
                  pl.BlockSpec(ws.shape, lambda i: (0, 0, 0)),
                  pl.BlockSpec(b_t.shape, lambda i: (0, 0))],
        out_specs=(pl.BlockSpec((tm, POOL_WIDTH), row2), pl.BlockSpec((tm, GMLP_WIDTH), row2)),
        out_shape=(jax.ShapeDtypeStruct((seq, POOL_WIDTH), BF16), jax.ShapeDtypeStruct((seq, GMLP_WIDTH), BF16)),
        compiler_params=_cparams(("parallel",), vmem), name="mix1_prompt",
    )(hp, hp, u, v16, pw16, ps, ws, b_t)


def _mix1_sample_kernel(ws_ref, b_ref, st_ref, hp_ref, u_ref, v_ref, pw_ref, ps_ref, po_ref, go_ref, *, t_s, past):
    for gi, w in enumerate(POOL_WINDOWS):
        cols = slice(gi * POOL_GROUP, (gi + 1) * POOL_GROUP)
        tails = [None] * w
        tail = None
        for k in range(1, w):
            row = st_ref[POOL_BUF - k, :, cols]
            tail = row if tail is None else tail + row
            tails[k] = tail
        for t in range(t_s):
            wsum = None
            for tp in range(max(0, t - w + 1), t + 1):
                r = hp_ref[tp, :, cols]
                wsum = r if wsum is None else wsum + r
            nbuf = max(w - 1 - t, 0)
            if nbuf > 0:
                wsum = wsum + tails[nbuf]
            cnt = float(min(past + t + 1, w))
            d = (wsum / cnt - hp_ref[t, :, cols]).astype(BF16)
            po_ref[t, :, cols] = (_dot(d, pw_ref[gi]) * ps_ref[:, cols]).astype(BF16)
    for g in range(GMLP_GROUPS):
        cols = slice(g * GMLP_GROUP, (g + 1) * GMLP_GROUP)
        for t in range(t_s):
            mixed = None
            for j in range(t + 1):
                term = ws_ref[g, t, j] * v_ref[j, :, cols].astype(F32)
                mixed = term if mixed is None else mixed + term
            mixed = mixed + b_ref[g, t]
            go_ref[t, :, cols] = (u_ref[t, :, cols] * mixed).astype(BF16)


def _mix1_sample(ws_small, b_small, st_t, hp_t, u_t, v16_t, pw16, ps, *, t_s, past):
    b = hp_t.shape[1]
    full = lambda a: pl.BlockSpec(a.shape, lambda i, n=a.ndim: (0,) * n)
    smem = lambda a: pl.BlockSpec(a.shape, lambda i, n=a.ndim: (0,) * n, memory_space=pltpu.SMEM)
    args = (ws_small, b_small, st_t, hp_t, u_t, v16_t, pw16, ps)
    in_specs = [smem(ws_small), smem(b_small)] + [full(a) for a in args[2:]]
    vmem = 4 * (st_t.size * 4 + 3 * hp_t.size * 4) + 8 * 1024 * 1024
    return pl.pallas_call(
        functools.partial(_mix1_sample_kernel, t_s=t_s, past=past), grid=(1,),
        in_specs=in_specs,
        out_specs=(pl.BlockSpec((t_s, b, POOL_WIDTH), lambda i: (0, 0, 0)),
                   pl.BlockSpec((t_s, b, GMLP_WIDTH), lambda i: (0, 0, 0))),
        out_shape=(jax.ShapeDtypeStruct((t_s, b, POOL_WIDTH), BF16),
                   jax.ShapeDtypeStruct((t_s, b, GMLP_WIDTH), BF16)),
        compiler_params=_cparams(("arbitrary",), vmem), name="mix1_sample",
    )(*args)


def _pick_tile(n, prefs):
    for t in prefs:
        if n % t == 0:
            return t
    raise ValueError(f"no tile in {prefs} divides {n}")


def kernel(x_prompt, x_sample, cache_mla_ckv, cache_mla_kpe, cache_diff_k, cache_diff_v, state_pool, page_table, norm_gains, w_up, w_down, mla_diff_w_in, mla_q_norm, mla_w_uq, mla_kv_norm, mla_w_uk, mla_w_uv, diff_lambda, diff_subln, mla_diff_w_out, pool_gmlp_w_in, pool_w, pool_scale, gmlp_norm, gmlp_ws, gmlp_b, pool_gmlp_w_out):
    bp, seq, d = x_prompt.shape
    b, t_s, _ = x_sample.shape
    assert bp == 1, "one prompt sequence"
    n_pages = page_table.shape[1]
    page = cache_mla_ckv.shape[2]
    past = n_pages * page
    n = seq + b * t_s
    depth = norm_gains.shape[0]
    assert t_s <= 16 and (MLA_HEADS * t_s) % 16 == 0

    ns = b * t_s
    tm = _pick_tile(math.gcd(seq, ns), (256, 128))
    tm_mlp = _pick_tile(n, (512, 256, 128))
    tf = _pick_tile(w_up.shape[2], (1024, 512))
    tq_d = _pick_tile(seq, (128,))
    tk_d = _pick_tile(seq, (512, 256, 128))
    tq_m = _pick_tile(seq, (512, 256, 128))
    pages = _pick_tile(n_pages, (16, 8, 4, 2, 1))

    xs = (x_prompt.reshape(seq, d), x_sample.reshape(ns, d))
    wup16 = w_up.astype(BF16)
    wdn16 = w_down.astype(BF16)
    half = MLA_ROPE // 2
    inv = ROPE_THETA ** (-jnp.arange(half, dtype=F32) / half)
    inv_tab = jnp.tile(inv, LANES // half).reshape(1, LANES)

    outs = {}
    for i in range(depth):
        g = norm_gains[i]
        j = i // 2
        if i % 2 == 0:
            lam_init = 0.8 - 0.6 * math.exp(-0.3 * i)
            w_in = mla_diff_w_in[j]
            o_cq, o_ckv, o_kpe = 0, Q_LORA, Q_LORA + KV_LORA
            o_dq = o_kpe + MLA_ROPE
            o_dk = o_dq + DIFF_HEADS * 2 * DIFF_HD
            o_dv = o_dk + DIFF_KV_HEADS * 2 * DIFF_HD
            win16 = jnp.concatenate(
                [w_in[:, o_cq:o_ckv], w_in[:, o_ckv:o_kpe], w_in[:, o_dq:], w_in[:, o_kpe:o_dq],
                 jnp.zeros((d, LANES - MLA_ROPE), F32)], axis=1).astype(BF16)
            wuq = mla_w_uq[j].reshape(Q_LORA, MLA_HEADS, MLA_NOPE + MLA_ROPE)
            wuq16 = jnp.concatenate([wuq[:, :, :MLA_NOPE].reshape(Q_LORA, -1),
                                     wuq[:, :, MLA_NOPE:].reshape(Q_LORA, -1)], axis=1).astype(BF16)
            wuk_t = jnp.transpose(mla_w_uk[j], (1, 2, 0)).astype(BF16)
            wuv_t = jnp.transpose(mla_w_uv[j], (1, 0, 2)).astype(BF16)
            wuk_f = mla_w_uk[j].reshape(KV_LORA, MLA_HEADS * MLA_NOPE).astype(BF16)
            wuv_f = mla_w_uv[j].reshape(KV_LORA, MLA_HEADS * MLA_V).astype(BF16)
            if len(xs) == 1:
                xs = (xs[0][:seq], xs[0][seq:])
            (qf, kf, vf, qlat, qpe, ckv16s, kpe16s, ckv32, kpe32, dq16, dk32, dk16, dv32, dv16) = _proj0(
                xs[0], xs[1], g[0:1], win16, mla_q_norm[j].reshape(1, -1), wuq16, mla_kv_norm[j].reshape(1, -1),
                wuk_t, wuk_f, wuv_f, inv_tab, past=past, t_s=t_s, tm=tm)
            lam_p = diff_lambda[j]
            sub = diff_subln[j].reshape(1, -1)

            omla_p = _mla_prompt(qf, kf, vf, tq=tq_m, tk=tq_m)

            qlat_s = jnp.transpose(qlat.reshape(MLA_HEADS, b, t_s, KV_LORA), (1, 0, 2, 3)).reshape(b, MLA_HEADS * t_s, KV_LORA)
            qpe_s = jnp.transpose(qpe.reshape(MLA_HEADS, b, t_s, MLA_ROPE), (1, 0, 2, 3)).reshape(b, MLA_HEADS * t_s, MLA_ROPE)
            dq_r = dq16[seq:].reshape(b, t_s, DIFF_KV_HEADS, DIFF_REP, 2, DIFF_HD)
            dq_r = jnp.transpose(dq_r, (0, 2, 4, 3, 1, 5))
            zq = jnp.zeros_like(dq_r[:, :, 0])
            dq_s = jnp.stack([jnp.concatenate([dq_r[:, :, 0], zq], axis=-1),
                              jnp.concatenate([zq, dq_r[:, :, 1]], axis=-1)], axis=2)
            dq_s = dq_s.reshape(b, DIFF_KV_HEADS * 2 * DIFF_REP * t_s, 2 * DIFF_HD)
            npad = 16

            def new_rows(a):
                a = a.reshape(b, t_s, a.shape[-1])
                return jnp.pad(a, ((0, 0), (0, npad - t_s), (0, 0)))

            def interleaved(a):
                return a.reshape(a.shape[:-2] + (a.shape[-2] * DIFF_KV_HEADS, a.shape[-1] // DIFF_KV_HEADS))

            n_pool = cache_diff_k.shape[1]
            olat_s, od_s, od_p = _sample_attn(
                page_table, qlat_s, qpe_s, dq_s, new_rows(ckv16s), new_rows(kpe16s),
                interleaved(new_rows(dk16[seq:])), interleaved(new_rows(dv16[seq:])),
                lam_p, sub, dq16, dk16, dv16, cache_mla_ckv, jnp.swapaxes(cache_mla_kpe, 2, 3),
                cache_diff_k.reshape(cache_diff_k.shape[0], n_pool, page * DIFF_KV_HEADS, 2 * DIFF_HD),
                cache_diff_v.reshape(cache_diff_v.shape[0], n_pool, page * DIFF_KV_HEADS, DIFF_VD),
                seq=seq, layer=j, pages=pages, t_s=t_s, lam_init=lam_init, tq=tq_d, tk=tk_d)
            olat_t = jnp.transpose(olat_s.reshape(b, MLA_HEADS, t_s, KV_LORA), (1, 0, 2, 3)).reshape(MLA_HEADS, b * t_s, KV_LORA)
            omla_s = _head_proj(olat_t, wuv_t)
            od_s = jnp.transpose(od_s.reshape(b, DIFF_KV_HEADS, DIFF_REP, t_s, DIFF_VD), (0, 3, 1, 2, 4)).reshape(b * t_s, DIFF_HEADS * DIFF_VD)
            mixed = (omla_p, omla_s, od_p, od_s)
            w_out16 = mla_diff_w_out[j].astype(BF16)

            outs.setdefault("ckv_p", []).append(ckv32[:seq].reshape(bp, seq, KV_LORA))
            outs.setdefault("kpe_p", []).append(kpe32[:seq].reshape(bp, seq, MLA_ROPE))
            outs.setdefault("k_p", []).append(dk32[:seq].reshape(bp, seq, DIFF_KV_HEADS, 2 * DIFF_HD))
            outs.setdefault("v_p", []).append(dv32[:seq].reshape(bp, seq, DIFF_KV_HEADS, DIFF_VD))
            outs.setdefault("ckv_s", []).append(ckv32[seq:].reshape(b, t_s, KV_LORA))
            outs.setdefault("kpe_s", []).append(kpe32[seq:].reshape(b, t_s, MLA_ROPE))
            outs.setdefault("k_s", []).append(dk32[seq:].reshape(b, t_s, DIFF_KV_HEADS, 2 * DIFF_HD))
            outs.setdefault("v_s", []).append(dv32[seq:].reshape(b, t_s, DIFF_KV_HEADS, DIFF_VD))
        else:
            if len(xs) == 2:
                xs = (jnp.concatenate(xs, axis=0),)
            hp, u, v32, v16 = _proj1(xs[0], g[0:1], pool_gmlp_w_in[j].astype(BF16), gmlp_norm[j].reshape(1, -1),
                                     tm=tm)
            pw16 = pool_w[j].astype(BF16)
            ps = pool_scale[j].reshape(1, -1)
            tm1 = _pick_tile(seq, (256, 128))
            po_p, go_p = _mix1_prompt(hp, u, v16, pw16, ps, gmlp_ws[j], gmlp_b[j].T, seq=seq, tm=tm1)

            def steps_first(a):
                return jnp.transpose(a[seq:].reshape(b, t_s, a.shape[-1]), (1, 0, 2))

            def rows_first(a):
                return jnp.transpose(a, (1, 0, 2)).reshape(ns, a.shape[-1])

            ws_small = gmlp_ws[j][:, :t_s, :t_s].astype(BF16).astype(F32)
            po_s, go_s = _mix1_sample(ws_small, gmlp_b[j][:, :t_s], jnp.transpose(state_pool[j], (1, 0, 2)),
                                      steps_first(hp), steps_first(u), steps_first(v16), pw16, ps, t_s=t_s, past=past)
            mixed = (po_p, rows_first(po_s), go_p, rows_first(go_s))
            w_out16 = pool_gmlp_w_out[j].astype(BF16)

            hp_s = hp[seq:].reshape(b, t_s, POOL_WIDTH)
            outs.setdefault("pool_p", []).append(hp[seq - POOL_BUF:seq].reshape(bp, POOL_BUF, POOL_WIDTH))
            outs.setdefault("pool_s", []).append(jnp.concatenate([state_pool[j], hp_s], axis=1)[:, -POOL_BUF:])
            outs.setdefault("gv_s", []).append(v32[seq:].reshape(b, t_s, GMLP_WIDTH))

        x = _out_proj(*mixed, w_out16, g[1:2], xs, tm=tm)
        xs = (_mlp(x, g[2:3], g[3:4], wup16, wdn16, layer=i, tm=tm_mlp, tf=tf),)

    x = xs[0]
    st = lambda key: jnp.stack(outs[key])
    return (x[:seq].reshape(bp, seq, d), x[seq:].reshape(b, t_s, d),
            st("ckv_p"), st("kpe_p"), st("k_p"), st("v_p"), st("pool_p"),
            st("ckv_s"), st("kpe_s"), st("k_s"), st("v_s"), st("pool_s"), st("gv_s"))
```
